```python
import math
import jax, jax.numpy as jnp
from jax import lax
import numpy as np

D_MODEL = 1024
BATCH = 32
SEQ = 2048
DEPTH = 2
DEC_BATCH = 16
DEC_SEQ = 64
PAST_LEN = 2048

CHUNK = 64
Q_BLOCK = 128
N_HEADS = 4
QK_HEAD_DIM = 64
V_HEAD_DIM = 2 * QK_HEAD_DIM
ATTN_WIDTH = N_HEADS * V_HEAD_DIM
CONV_WIDTH = D_MODEL - ATTN_WIDTH
MIX_WIDTH = ATTN_WIDTH + CONV_WIDTH
QK_WIDTH = N_HEADS * 2 * QK_HEAD_DIM
CONV_KERNEL = 31
CONV_STATE = CONV_KERNEL - 1
ROPE_THETA = 500000.0
ROT_DIM = QK_HEAD_DIM // 4
NORM_EPS = 1e-6
SUBLN_EPS = 1e-5
LN_EPS = 1e-5
_SPLITS = (QK_WIDTH, 2 * QK_WIDTH, 2 * QK_WIDTH + ATTN_WIDTH, 2 * QK_WIDTH + 2 * ATTN_WIDTH,
           2 * QK_WIDTH + 2 * ATTN_WIDTH + 2 * CONV_WIDTH)
IN_COLS = 2 * QK_WIDTH + 2 * ATTN_WIDTH + 3 * CONV_WIDTH

kernel_name = "hybrid_diffattn_conformer_stream_step"


def _rms_norm(x, g, eps):
    xf = x.astype(jnp.float32)
    y = xf * lax.rsqrt(jnp.mean(xf * xf, axis=-1, keepdims=True) + eps)
    return (y * g.astype(jnp.float32)).astype(x.dtype)


def _layer_norm(x, g, b, eps):
    xf = x.astype(jnp.float32)
    mu = jnp.mean(xf, axis=-1, keepdims=True)
    xc = xf - mu
    y = xc * lax.rsqrt(jnp.mean(xc * xc, axis=-1, keepdims=True) + eps)
    return (y * g.astype(jnp.float32) + b.astype(jnp.float32)).astype(x.dtype)


def _rope(x, posf):
    inv = jnp.float32(ROPE_THETA) ** (-jnp.arange(0, ROT_DIM, 2, dtype=jnp.float32) / ROT_DIM)
    ang = posf[:, None] * inv[None, :]
    cos = jnp.cos(ang)[None, :, None, None, :]
    sin = jnp.sin(ang)[None, :, None, None, :]
    xr = x[..., :ROT_DIM].astype(jnp.float32)
    x1, x2 = xr[..., : ROT_DIM // 2], xr[..., ROT_DIM // 2:]
    rot = jnp.concatenate([x1 * cos - x2 * sin, x2 * cos + x1 * sin], axis=-1)
    return jnp.concatenate([rot.astype(x.dtype), x[..., ROT_DIM:]], axis=-1)


def _diff_attn_block(q, k, v, qpos, kpos, lam):
    s = jnp.einsum('bqhmd,bkhmd->bhmqk', q.astype(jnp.float32), k.astype(jnp.float32)) * (QK_HEAD_DIM ** -0.5)
    allowed = (kpos[None, :] // CHUNK) <= (qpos[:, None] // CHUNK)
    s = jnp.where(allowed, s, -jnp.inf)
    p = jax.nn.softmax(s, axis=-1)
    a = p[:, :, 0] - lam * p[:, :, 1]
    return jnp.einsum('bhqk,bkhd->bqhd', a, v.astype(jnp.float32))


def _layer(x, c, pos, k_past, v_past, conv_past, layer_idx,
           w_ada, b_ada, g_pre, g_post, w_in, w_out,
           lam_q1, lam_k1, lam_q2, lam_k2, g_subln, w_dw, b_dw, g_ln, b_ln):
    B, S, _ = x.shape
    mod = jax.nn.silu(c) @ w_ada + b_ada
    shift, scale, gate = jnp.split(mod, 3, axis=-1)
    h = _rms_norm(x, g_pre, NORM_EPS) * (1 + scale[:, None, :]) + shift[:, None, :]
    z = h @ w_in
    q, k, v, ga, glu, gc = jnp.split(z, _SPLITS, axis=-1)
    q = q.reshape(B, S, N_HEADS, 2, QK_HEAD_DIM)
    k = k.reshape(B, S, N_HEADS, 2, QK_HEAD_DIM)
    v = v.reshape(B, S, N_HEADS, V_HEAD_DIM)
    posf = pos.astype(jnp.float32)
    q = _rope(q, posf)
    k = _rope(k, posf)

    lam_init = 0.8 - 0.6 * math.exp(-0.3 * layer_idx)
    lam = (jnp.exp(jnp.sum(lam_q1.astype(jnp.float32) * lam_k1.astype(jnp.float32)))
           - jnp.exp(jnp.sum(lam_q2.astype(jnp.float32) * lam_k2.astype(jnp.float32))) + lam_init)

    if k_past is None:
        outs = []
        for s0 in range(0, S, Q_BLOCK):
            e = min(s0 + Q_BLOCK, S)
            outs.append(_diff_attn_block(q[:, s0:e], k[:, :e], v[:, :e], pos[s0:e], pos[:e], lam))
        o = jnp.concatenate(outs, axis=1)
    else:
        k_all = jnp.concatenate([k_past.astype(k.dtype), k], axis=1)
        v_all = jnp.concatenate([v_past.astype(v.dtype), v], axis=1)
        kpos = jnp.arange(k_all.shape[1], dtype=jnp.int32)
        o = _diff_attn_block(q, k_all, v_all, pos, kpos, lam)
    o = _rms_norm(o, g_subln, SUBLN_EPS) * (1.0 - lam_init)
    attn_out = o.astype(x.dtype).reshape(B, S, ATTN_WIDTH) * jax.nn.silu(ga)

    ga_lin, gb = jnp.split(glu, 2, axis=-1)
    u = ga_lin * jax.nn.sigmoid(gb)
    if conv_past is None:
        past = jnp.zeros((B, CONV_STATE, CONV_WIDTH), u.dtype)
    else:
        past = conv_past.astype(u.dtype)
    u_pad = jnp.concatenate([past, u], axis=1)
    conv = lax.conv_general_dilated(u_pad, w_dw[:, None, :].astype(u.dtype), (1,), 'VALID',
                                    dimension_numbers=('NWC', 'WIO', 'NWC'),
                                    feature_group_count=CONV_WIDTH) + b_dw
    conv_out = jax.nn.silu(_layer_norm(conv, g_ln, b_ln, LN_EPS)) * jax.nn.silu(gc)

    m = jnp.concatenate([attn_out, conv_out], axis=-1) @ w_out
    y = x + gate[:, None, :] * _rms_norm(m, g_post, NORM_EPS)
    return y, k, v, u_pad[:, -CONV_STATE:]


def setup_inputs(seed: int = 0) -> dict:
    key = jax.random.key(seed)
    ks = jax.random.split(key, 24)
    f32 = jnp.float32
    nrm = lambda k, shape, s: jax.random.normal(k, shape, f32) * s
    return {
        "x_prompt": nrm(ks[0], (BATCH, SEQ, D_MODEL), 1.0),
        "x_sample": nrm(ks[1], (DEC_BATCH, DEC_SEQ, D_MODEL), 1.0),
        "c_prompt": nrm(ks[2], (BATCH, D_MODEL), 1.0),
        "c_sample": nrm(ks[3], (DEC_BATCH, D_MODEL), 1.0),
        "cache_k": nrm(ks[4], (DEPTH, DEC_BATCH, PAST_LEN, N_HEADS, 2, QK_HEAD_DIM), 1.0),
        "cache_v": nrm(ks[5], (DEPTH, DEC_BATCH, PAST_LEN, N_HEADS, V_HEAD_DIM), 1.0),
        "state_conv": nrm(ks[6], (DEPTH, DEC_BATCH, CONV_STATE, CONV_WIDTH), 0.5),
        "w_ada": nrm(ks[7], (DEPTH, D_MODEL, 3 * D_MODEL), 0.5 * D_MODEL ** -0.5),
        "b_ada": nrm(ks[8], (DEPTH, 3 * D_MODEL), 0.01),
        "g_pre": 1.0 + nrm(ks[9], (DEPTH, D_MODEL), 0.01),
        "g_post": 1.0 + nrm(ks[10], (DEPTH, D_MODEL), 0.01),
        "w_in": nrm(ks[11], (DEPTH, D_MODEL, IN_COLS), D_MODEL ** -0.5),
        "w_out": nrm(ks[12], (DEPTH, MIX_WIDTH, D_MODEL), MIX_WIDTH ** -0.5),
        "lam_q1": nrm(ks[13], (DEPTH, QK_HEAD_DIM), 0.1),
        "lam_k1": nrm(ks[14], (DEPTH, QK_HEAD_DIM), 0.1),
        "lam_q2": nrm(ks[15], (DEPTH, QK_HEAD_DIM), 0.1),
        "lam_k2": nrm(ks[16], (DEPTH, QK_HEAD_DIM), 0.1),
        "g_subln": 1.0 + nrm(ks[17], (DEPTH, V_HEAD_DIM), 0.01),
        "w_dw": nrm(ks[18], (DEPTH, CONV_KERNEL, CONV_WIDTH), CONV_KERNEL ** -0.5),
        "b_dw": nrm(ks[19], (DEPTH, CONV_WIDTH), 0.01),
        "g_ln": 1.0 + nrm(ks[20], (DEPTH, CONV_WIDTH), 0.01),
        "b_ln": nrm(ks[21], (DEPTH, CONV_WIDTH), 0.01),
    }


def reference(x_prompt, x_sample, c_prompt, c_sample, cache_k, cache_v, state_conv,
              w_ada, b_ada, g_pre, g_post, w_in, w_out,
              lam_q1, lam_k1, lam_q2, lam_k2, g_subln, w_dw, b_dw, g_ln, b_ln):
    pos_p = jnp.arange(x_prompt.shape[1], dtype=jnp.int32)
    pos_s = PAST_LEN + jnp.arange(x_sample.shape[1], dtype=jnp.int32)
    yp, ys = x_prompt, x_sample
    kp, vp, cp, kq, vq, cq = [], [], [], [], [], []
    for l in range(DEPTH):
        w = (w_ada[l], b_ada[l], g_pre[l], g_post[l], w_in[l], w_out[l],
             lam_q1[l], lam_k1[l], lam_q2[l], lam_k2[l], g_subln[l], w_dw[l], b_dw[l], g_ln[l], b_ln[l])
        yp, k1, v1, s1 = _layer(yp, c_prompt, pos_p, None, None, None, l, *w)
        ys, k2, v2, s2 = _layer(ys, c_sample, pos_s, cache_k[l], cache_v[l], state_conv[l], l, *w)
        kp.append(k1); vp.append(v1); cp.append(s1)
        kq.append(k2); vq.append(v2); cq.append(s2)
    return (yp, ys, jnp.stack(kp), jnp.stack(vp), jnp.stack(cp),
            jnp.stack(kq), jnp.stack(vq), jnp.stack(cq))
```

```python
import functools
import math

import jax
import jax.numpy as jnp
from jax import lax
from jax.experimental import pallas as pl
from jax.experimental.pallas import tpu as pltpu

F32 = jnp.float32
BF16 = jnp.bfloat16

D_MODEL = 1024
N_HEADS = 4
QK_HEAD_DIM = 64
V_HEAD_DIM = 128
ATTN_WIDTH = N_HEADS * V_HEAD_DIM
QK_WIDTH = N_HEADS * 2 * QK_HEAD_DIM
CONV_WIDTH = D_MODEL - ATTN_WIDTH
CONV_KERNEL = 31
CONV_STATE = CONV_KERNEL - 1
CHUNK = 64
ROT_DIM = 16
ROPE_THETA = 500000.0
NORM_EPS = 1e-6
SUBLN_EPS = 1e-5
LN_EPS = 1e-5
IN_COLS = 2 * QK_WIDTH + 2 * ATTN_WIDTH + 3 * CONV_WIDTH
COL_Q, COL_K, COL_V, COL_GA = 0, QK_WIDTH, 2 * QK_WIDTH, 2 * QK_WIDTH + ATTN_WIDTH
COL_GLU_A = 2 * QK_WIDTH + 2 * ATTN_WIDTH
COL_GLU_B = COL_GLU_A + CONV_WIDTH
COL_GC = COL_GLU_B + CONV_WIDTH

LANES = 128
SUBLANES = 8
CONV_PAD = 32
MASK_VALUE = -1e30
PROMPT_TILE = 256
KEY_TILE = 256
VMEM_LIMIT_BYTES = 60000 * 1024


def _silu(x):
    return x * jax.nn.sigmoid(x)


def _dot(a, b):
    return jnp.dot(a, b, preferred_element_type=F32)


def _dot_nt(a, b):
    return lax.dot_general(a, b, (((1,), (1,)), ((), ())), preferred_element_type=F32)


def _rope(z, cos_t, sa_t, sb_t):
    blocks = []
    for c in range(z.shape[1] // LANES):
        zc = z[:, c * LANES:(c + 1) * LANES]
        up = pltpu.roll(zc, LANES - ROT_DIM // 2, 1)
        down = pltpu.roll(zc, ROT_DIM // 2, 1)
        blocks.append(zc * cos_t + up * sa_t + down * sb_t)
    return jnp.concatenate(blocks, axis=1)


def _conv_taps(u_s, sl, rows, wdw_ref, bdw):
    base = CONV_PAD - CONV_STATE
    row_block = min(rows, 128)
    out_rows = []
    for rb in range(rows // row_block):
        cols = []
        for c in range(CONV_WIDTH // LANES):
            lanes = slice(c * LANES, (c + 1) * LANES)
            acc = jnp.broadcast_to(bdw[:, lanes], (row_block, LANES))
            for r in range(SUBLANES):
                offs = [o for o in range(base, base + CONV_KERNEL) if o % SUBLANES == r]
                lo, hi = offs[0], offs[-1]
                win = u_s[sl, pl.ds(rb * row_block + lo, row_block + hi - lo), lanes]
                for o in offs:
                    w_row = wdw_ref[sl, o - base:o - base + 1, lanes]
                    acc = acc + w_row * win[o - lo:o - lo + row_block]
            cols.append(acc)
        out_rows.append(jnp.concatenate(cols, axis=1))
    return jnp.concatenate(out_rows, axis=0)


def _attention(q_s, k0b, k1b, vb, mask, n_past, key_tile, k0_s, k1_s, v_s, kl, acc_s, m_s, l_s):
    for h in range(N_HEADS):
        hs = slice(h * V_HEAD_DIM, (h + 1) * V_HEAD_DIM)
        qh = q_s[:, hs]
        for mp in range(2):
            idx = 2 * h + mp
            s = _dot_nt(qh, (k0b, k1b)[mp][:, hs])
            if mask is not None:
                s = jnp.where(mask, s, MASK_VALUE)
            mx = jnp.max(s, axis=1, keepdims=True)
            p = jnp.exp(s - mx)
            m_s[idx] = mx
            l_s[idx] = jnp.sum(p, axis=1, keepdims=True)
            acc_s[idx] = _dot(p.astype(BF16), vb[:, hs])

    def body(j, carry):
        r0 = pl.multiple_of(j * key_tile, key_tile)
        for h in range(N_HEADS):
            hs = slice(h * V_HEAD_DIM, (h + 1) * V_HEAD_DIM)
            qh = q_s[:, hs]
            vt = v_s[kl, pl.ds(r0, key_tile), hs]
            for mp in range(2):
                idx = 2 * h + mp
                kt = (k0_s, k1_s)[mp][kl, pl.ds(r0, key_tile), hs]
                s = _dot_nt(qh, kt)
                m_old = m_s[idx]
                m_new = jnp.maximum(m_old, jnp.max(s, axis=1, keepdims=True))
                alpha = jnp.exp(m_old - m_new)
                p = jnp.exp(s - m_new)
                l_s[idx] = alpha * l_s[idx] + jnp.sum(p, axis=1, keepdims=True)
                acc_s[idx] = alpha * acc_s[idx] + _dot(p.astype(BF16), vt)
                m_s[idx] = m_new
        return carry

    lax.fori_loop(0, n_past, body, 0)


def _mixer_layer(x, sl, lam_init, shift, scale, gate, prm, tabs, u_s, q_s, att, past, new_kv_out):
    rows = x.shape[0]
    acc_s, m_s, l_s = att

    ms = jnp.mean(x * x, axis=-1, keepdims=True)
    h = (x * lax.rsqrt(ms + NORM_EPS)) * (prm["g_pre"][sl] * (1.0 + scale)) + shift
    hb = h.astype(BF16)
    w_in = prm["w_in"]

    def proj(col, width):
        return _dot(hb, w_in[sl, :, col:col + width])

    cos_t, sa_t, sb_t = tabs
    q = _rope(proj(COL_Q, QK_WIDTH), cos_t, sa_t, sb_t)
    q_s[...] = (q * (QK_HEAD_DIM ** -0.5)).astype(BF16)
    k = _rope(proj(COL_K, QK_WIDTH), cos_t, sa_t, sb_t)
    v = proj(COL_V, ATTN_WIDTH)
    first_map = (lax.broadcasted_iota(jnp.int32, (rows, QK_WIDTH), 1) & (V_HEAD_DIM - 1)) < QK_HEAD_DIM
    k0b = jnp.where(first_map, k, 0.0).astype(BF16)
    k1b = jnp.where(first_map, 0.0, k).astype(BF16)
    vb = v.astype(BF16)
    new_kv_out(k, v, k0b, k1b, vb)

    mask, n_past, key_tile, k0_s, k1_s, v_s, kl = past
    _attention(q_s, k0b, k1b, vb, mask, n_past, key_tile, k0_s, k1_s, v_s, kl, acc_s, m_s, l_s)

    lam_p = prm["lam"][sl]
    lam = (jnp.exp(jnp.sum(lam_p[0:1] * lam_p[1:2], axis=1, keepdims=True))
           - jnp.exp(jnp.sum(lam_p[2:3] * lam_p[3:4], axis=1, keepdims=True)) + lam_init)
    ga = proj(COL_GA, ATTN_WIDTH)
    g_sub = prm["g_subln"][sl]
    heads = []
    for hd in range(N_HEADS):
        hs = slice(hd * V_HEAD_DIM, (hd + 1) * V_HEAD_DIM)
        o = acc_s[2 * hd] * (1.0 / l_s[2 * hd]) - acc_s[2 * hd + 1] * (lam / l_s[2 * hd + 1])
        o = o * lax.rsqrt(jnp.mean(o * o, axis=-1, keepdims=True) + SUBLN_EPS) * g_sub
        heads.append(((o * (1.0 - lam_init)) * _silu(ga[:, hs])).astype(BF16))
    attn_out = jnp.concatenate(heads, axis=1)

    u = proj(COL_GLU_A, CONV_WIDTH) * jax.nn.sigmoid(proj(COL_GLU_B, CONV_WIDTH))
    u_s[sl, CONV_PAD:CONV_PAD + rows, :] = u
    conv = _conv_taps(u_s, sl, rows, prm["w_dw"], prm["b_dw"][sl])
    mu = jnp.mean(conv, axis=-1, keepdims=True)
    xc = conv - mu
    ln = xc * lax.rsqrt(jnp.mean(xc * xc, axis=-1, keepdims=True) + LN_EPS) * prm["g_ln"][sl] + prm["b_ln"][sl]
    conv_out = (_silu(ln) * _silu(proj(COL_GC, CONV_WIDTH))).astype(BF16)

    w_out = prm["w_out"]
    m = _dot(attn_out, w_out[sl, 0:ATTN_WIDTH, :]) + _dot(conv_out, w_out[sl, ATTN_WIDTH:ATTN_WIDTH + CONV_WIDTH, :])
    mn = m * lax.rsqrt(jnp.mean(m * m, axis=-1, keepdims=True) + NORM_EPS) * prm["g_post"][sl]
    return x + gate * mn


def _lam_init(layer_idx):
    return 0.8 - 0.6 * math.exp(-0.3 * layer_idx)


_PRM_NAMES = ("g_pre", "g_post", "w_in", "w_out", "lam", "g_subln", "w_dw", "b_dw", "g_ln", "b_ln")


def _prompt_kernel(depth, tile, x_ref, mod_ref, cos_ref, sa_ref, sb_ref, *rest):
    prm = dict(zip(_PRM_NAMES, rest[:len(_PRM_NAMES)]))
    y_ref, k_ref, v_ref, conv_ref = rest[len(_PRM_NAMES):len(_PRM_NAMES) + 4]
    k0_s, k1_s, v_s, u_s, q_s, acc_s, m_s, l_s = rest[len(_PRM_NAMES) + 4:]
    i = pl.program_id(1)
    row0 = pl.multiple_of(i * tile, tile)

    @pl.when(i == 0)
    def _():
        u_s[:, 0:CONV_PAD, :] = jnp.zeros((depth, CONV_PAD, CONV_WIDTH), F32)

    tabs = (cos_ref[...], sa_ref[...], sb_ref[...])
    q_chunk = lax.shift_right_logical(lax.broadcasted_iota(jnp.int32, (tile, tile), 0), int(math.log2(CHUNK)))
    k_chunk = lax.shift_right_logical(lax.broadcasted_iota(jnp.int32, (tile, tile), 1), int(math.log2(CHUNK)))
    mask = k_chunk <= q_chunk

    x = x_ref[0]
    for l in range(depth):
        def new_kv_out(k, v, k0b, k1b, vb, l=l):
            k_ref[l, 0] = k
            v_ref[l, 0] = v
            k0_s[l, pl.ds(row0, tile), :] = k0b
            k1_s[l, pl.ds(row0, tile), :] = k1b
            v_s[l, pl.ds(row0, tile), :] = vb

        x = _mixer_layer(x, l, _lam_init(l), mod_ref[l, 0, 0:1, :], mod_ref[l, 0, 1:2, :], mod_ref[l, 0, 2:3, :],
                         prm, tabs, u_s, q_s, (acc_s, m_s, l_s), (mask, i, tile, k0_s, k1_s, v_s, l), new_kv_out)
        conv_ref[l, 0] = u_s[l, tile + CONV_PAD - CONV_STATE:tile + CONV_PAD, :]
        u_s[l, 0:CONV_PAD, :] = u_s[l, tile:tile + CONV_PAD, :]
    y_ref[0] = x


def _sample_kernel(depth, rows, past_len, x_ref, mod_ref, cos_ref, sa_ref, sb_ref, ck_ref, cv_ref, cs_ref, *rest):
    prm = dict(zip(_PRM_NAMES, rest[:len(_PRM_NAMES)]))
    y_ref, k_ref, v_ref, conv_ref = rest[len(_PRM_NAMES):len(_PRM_NAMES) + 4]
    k0_s, k1_s, v_s, u_s, q_s, acc_s, m_s, l_s, xin_s, y1_s = rest[len(_PRM_NAMES) + 4:]
    l = pl.program_id(0)
    b = pl.program_id(1)

    @pl.when(l == 0)
    def _():
        xin_s[...] = x_ref[0]

    @pl.when(l != 0)
    def _():
        xin_s[...] = y1_s[b]

    ck = ck_ref[0, 0]
    first_map = (lax.broadcasted_iota(jnp.int32, (past_len, QK_WIDTH), 1) & (V_HEAD_DIM - 1)) < QK_HEAD_DIM
    k0_s[0] = jnp.where(first_map, ck, 0.0).astype(BF16)
    k1_s[0] = jnp.where(first_map, 0.0, ck).astype(BF16)
    v_s[0] = cv_ref[0, 0].astype(BF16)
    u_s[0, 0:CONV_PAD, :] = jnp.zeros((CONV_PAD, CONV_WIDTH), F32)
    u_s[0, CONV_PAD - CONV_STATE:CONV_PAD, :] = cs_ref[0, 0]

    def new_kv_out(k, v, k0b, k1b, vb):
        k_ref[0, 0] = k
        v_ref[0, 0] = v

    lam_init = jnp.where(l == 0, _lam_init(0), _lam_init(1)).astype(F32)
    tabs = (cos_ref[...], sa_ref[...], sb_ref[...])
    y = _mixer_layer(xin_s[...], 0, lam_init, mod_ref[0, 0, 0:1, :], mod_ref[0, 0, 1:2, :], mod_ref[0, 0, 2:3, :],
                     prm, tabs, u_s, q_s, (acc_s, m_s, l_s),
                     (None, past_len // KEY_TILE, KEY_TILE, k0_s, k1_s, v_s, 0), new_kv_out)
    conv_ref[0, 0] = u_s[0, rows + CONV_PAD - CONV_STATE:rows + CONV_PAD, :]
    y1_s[b] = y

    @pl.when(l == depth - 1)
    def _():
        y_ref[0] = y


def _mod_kernel(c_ref, w_ref, b_ref, o_ref):
    c = c_ref[...]
    o_ref[0] = _dot(_silu(c), w_ref[0]) + b_ref[0]


def _rope_tables(pos):
    inv = jnp.float32(ROPE_THETA) ** (-jnp.arange(0, ROT_DIM, 2, dtype=F32) / ROT_DIM)
    lane = jnp.arange(LANES, dtype=jnp.int32) % QK_HEAD_DIM
    ang = pos.astype(F32)[:, None] * inv[lane % (ROT_DIM // 2)][None, :]
    cos, sin = jnp.cos(ang), jnp.sin(ang)
    lo = (lane < ROT_DIM // 2)[None, :]
    hi = ((lane >= ROT_DIM // 2) & (lane < ROT_DIM))[None, :]
    return (jnp.where(lo | hi, cos, 1.0), jnp.where(lo, -sin, 0.0), jnp.where(hi, sin, 0.0))


def _const_spec(shape, single_buffer=False):
    nd = len(shape)
    kw = {"pipeline_mode": pl.Buffered(1)} if single_buffer else {}
    return pl.BlockSpec(shape, lambda *_: (0,) * nd, **kw)


def kernel(x_prompt, x_sample, c_prompt, c_sample, cache_k, cache_v, state_conv, w_ada, b_ada, g_pre, g_post, w_in,
           w_out, lam_q1, lam_k1, lam_q2, lam_k2, g_subln, w_dw, b_dw, g_ln, b_ln):
    batch, seq, _ = x_prompt.shape
    dec_batch, dec_seq, _ = x_sample.shape
    depth = w_in.shape[0]
    past_len = cache_k.shape[2]
    tile = min(PROMPT_TILE, seq)
    assert seq % tile == 0 and tile % CHUNK == 0 and past_len % KEY_TILE == 0 and dec_seq % SUBLANES == 0

    c_all = jnp.concatenate([c_prompt, c_sample], axis=0)
    n_c = c_all.shape[0]
    mod_cols = 512
    mod = pl.pallas_call(
        _mod_kernel,
        grid=(depth, 3 * D_MODEL // mod_cols),
        in_specs=[pl.BlockSpec((n_c, D_MODEL), lambda l, n: (0, 0)),
                  pl.BlockSpec((1, D_MODEL, mod_cols), lambda l, n: (l, 0, n)),
                  pl.BlockSpec((1, 1, mod_cols), lambda l, n: (l, 0, n))],
        out_specs=pl.BlockSpec((1, n_c, mod_cols), lambda l, n: (l, 0, n)),
        out_shape=jax.ShapeDtypeStruct((depth, n_c, 3 * D_MODEL), F32),
        name="adaln_mod",
    )(c_all, w_ada, b_ada.reshape(depth, 1, 3 * D_MODEL))
    mod_p = mod[:, :batch].reshape(depth, batch, 3, D_MODEL)
    mod_s = mod[:, batch:].reshape(depth, dec_batch, 3, D_MODEL)

    w_dw_p = jnp.concatenate([w_dw, jnp.zeros((depth, CONV_PAD - CONV_KERNEL, CONV_WIDTH), w_dw.dtype)], axis=1)
    params = (g_pre.reshape(depth, 1, D_MODEL), g_post.reshape(depth, 1, D_MODEL), w_in.astype(BF16),
              w_out.astype(BF16), jnp.stack([lam_q1, lam_k1, lam_q2, lam_k2], axis=1),
              g_subln.reshape(depth, 1, V_HEAD_DIM), w_dw_p, b_dw.reshape(depth, 1, CONV_WIDTH),
              g_ln.reshape(depth, 1, CONV_WIDTH), b_ln.reshape(depth, 1, CONV_WIDTH))

    tabs_p = _rope_tables(jnp.arange(seq, dtype=jnp.int32))
    tab_spec = pl.BlockSpec((tile, LANES), lambda b, i: (i, 0))
    kv_spec = pl.BlockSpec((depth, 1, tile, QK_WIDTH), lambda b, i: (0, b, i, 0))
    yp, kp, vp, cp = pl.pallas_call(
        functools.partial(_prompt_kernel, depth, tile),
        grid=(batch, seq // tile),
        in_specs=[pl.BlockSpec((1, tile, D_MODEL), lambda b, i: (b, i, 0)),
                  pl.BlockSpec((depth, 1, 3, D_MODEL), lambda b, i: (0, b, 0, 0)),
                  tab_spec, tab_spec, tab_spec]
                 + [_const_spec(p.shape, single_buffer=True) for p in params],
        out_specs=[pl.BlockSpec((1, tile, D_MODEL), lambda b, i: (b, i, 0)), kv_spec, kv_spec,
                   pl.BlockSpec((depth, 1, CONV_STATE, CONV_WIDTH), lambda b, i: (0, b, 0, 0))],
        out_shape=[jax.ShapeDtypeStruct((batch, seq, D_MODEL), F32),
                   jax.ShapeDtypeStruct((depth, batch, seq, QK_WIDTH), F32),
                   jax.ShapeDtypeStruct((depth, batch, seq, ATTN_WIDTH), F32),
                   jax.ShapeDtypeStruct((depth, batch, CONV_STATE, CONV_WIDTH), F32)],
        scratch_shapes=[pltpu.VMEM((depth, seq, QK_WIDTH), BF16), pltpu.VMEM((depth, seq, QK_WIDTH), BF16),
                        pltpu.VMEM((depth, seq, ATTN_WIDTH), BF16),
                        pltpu.VMEM((depth, CONV_PAD + tile, CONV_WIDTH), F32),
                        pltpu.VMEM((tile, QK_WIDTH), BF16),
                        pltpu.VMEM((2 * N_HEADS, tile, V_HEAD_DIM), F32),
                        pltpu.VMEM((2 * N_HEADS, tile, 1), F32), pltpu.VMEM((2 * N_HEADS, tile, 1), F32)],
        compiler_params=pltpu.CompilerParams(dimension_semantics=("arbitrary", "arbitrary"),
                                             vmem_limit_bytes=VMEM_LIMIT_BYTES),
        name="prompt_layers",
    )(x_prompt, mod_p, *tabs_p, *params)

    tabs_s = _rope_tables(past_len + jnp.arange(dec_seq, dtype=jnp.int32))

    def layer_spec(shape):
        nd = len(shape)
        return pl.BlockSpec((1,) + tuple(shape[1:]), lambda l, b: (l,) + (0,) * (nd - 1))

    lb_spec = lambda shape: pl.BlockSpec((1, 1) + tuple(shape[2:]), lambda l, b: (l, b) + (0,) * (len(shape) - 2))
    ck = cache_k.reshape(depth, dec_batch, past_len, QK_WIDTH)
    cv = cache_v.reshape(depth, dec_batch, past_len, ATTN_WIDTH)
    row_spec = pl.BlockSpec((1, dec_seq, D_MODEL), lambda l, b: (b, 0, 0))
    stab_spec = pl.BlockSpec((dec_seq, LANES), lambda l, b: (0, 0))
    ys, ks, vs, cs = pl.pallas_call(
        functools.partial(_sample_kernel, depth, dec_seq, past_len),
        grid=(depth, dec_batch),
        in_specs=[row_spec, lb_spec(mod_s.shape), stab_spec, stab_spec, stab_spec,
                  lb_spec(ck.shape), lb_spec(cv.shape), lb_spec(state_conv.shape)]
                 + [layer_spec(p.shape) for p in params],
        out_specs=[pl.BlockSpec((1, dec_seq, D_MODEL), lambda l, b: (jnp.where(l == depth - 1, b, 0), 0, 0)),
                   lb_spec((depth, dec_batch, dec_seq, QK_WIDTH)),
                   lb_spec((depth, dec_batch, dec_seq, ATTN_WIDTH)), lb_spec(state_conv.shape)],
        out_shape=[jax.ShapeDtypeStruct((dec_batch, dec_seq, D_MODEL), F32),
                   jax.ShapeDtypeStruct((depth, dec_batch, dec_seq, QK_WIDTH), F32),
                   jax.ShapeDtypeStruct((depth, dec_batch, dec_seq, ATTN_WIDTH), F32),
                   jax.ShapeDtypeStruct(state_conv.shape, F32)],
        scratch_shapes=[pltpu.VMEM((1, past_len, QK_WIDTH), BF16), pltpu.VMEM((1, past_len, QK_WIDTH), BF16),
                        pltpu.VMEM((1, past_len, ATTN_WIDTH), BF16),
                        pltpu.VMEM((1, CONV_PAD + dec_seq, CONV_WIDTH), F32),
                        pltpu.VMEM((dec_seq, QK_WIDTH), BF16),
                        pltpu.VMEM((2 * N_HEADS, dec_seq, V_HEAD_DIM), F32),
                        pltpu.VMEM((2 * N_HEADS, dec_seq, 1), F32), pltpu.VMEM((2 * N_HEADS, dec_seq, 1), F32),
                        pltpu.VMEM((dec_seq, D_MODEL), F32), pltpu.VMEM((dec_batch, dec_seq, D_MODEL), F32)],
        compiler_params=pltpu.CompilerParams(dimension_semantics=("arbitrary", "arbitrary"),
                                             vmem_limit_bytes=VMEM_LIMIT_BYTES),
        name="sample_layers",
    )(x_sample, mod_s, *tabs_s, ck, cv, state_conv, *params)

    return (yp, ys,
            kp.reshape(depth, batch, seq, N_HEADS, 2, QK_HEAD_DIM), vp.reshape(depth, batch, seq, N_HEADS, V_HEAD_DIM),
            cp,
            ks.reshape(depth, dec_batch, dec_seq, N_HEADS, 2, QK_HEAD_DIM),
            vs.reshape(depth, dec_batch, dec_seq, N_HEADS, V_HEAD_DIM), cs)
```

```python
import functools
import math

import jax
import jax.numpy as jnp
from jax import lax
from jax.experimental import pallas as pl
from jax.experimental.pallas import tpu as pltpu

F32 = jnp.float32
BF16 = jnp.bfloat16

D_MODEL = 1024
N_HEADS = 4
QK_HEAD_DIM = 64
V_HEAD_DIM = 128
ATTN_WIDTH = N_HEADS * V_HEAD_DIM
QK_WIDTH = N_HEADS * 2 * QK_HEAD_DIM
CONV_WIDTH = D_MODEL - ATTN_WIDTH
CONV_KERNEL = 31
CONV_STATE = CONV_KERNEL - 1
CHUNK = 64
ROT_DIM = 16
ROT_HALF = ROT_DIM // 2
ROPE_THETA = 500000.0
NORM_EPS = 1e-6
SUBLN_EPS = 1e-5
LN_EPS = 1e-5
COL_Q, COL_K, COL_V, COL_GA = 0, QK_WIDTH, 2 * QK_WIDTH, 2 * QK_WIDTH + ATTN_WIDTH
COL_GLU_A = 2 * QK_WIDTH + 2 * ATTN_WIDTH
COL_GLU_B = COL_GLU_A + CONV_WIDTH
COL_GC = COL_GLU_B + CONV_WIDTH
IN_COLS = COL_GC + CONV_WIDTH
WT_Q, WT_K, WT_GA = 0, QK_WIDTH, 2 * QK_WIDTH
WR_V, WR_GLU_A, WR_GLU_B, WR_GC = 0, ATTN_WIDTH, ATTN_WIDTH + CONV_WIDTH, ATTN_WIDTH + 2 * CONV_WIDTH

LANES = 128
SUBLANES = 8
CONV_PAD = 32
MASK_VALUE = -1e30
LOG2E = math.log2(math.e)
PROMPT_TILE = 256
KEY_TILE = 256
VMEM_LIMIT_BYTES = 60000 * 1024


def _silu(x):
    return x * jax.nn.sigmoid(x)


def _dot(a, b):
    return jnp.dot(a, b, preferred_element_type=F32)


def _dot_nt(a, b):
    return lax.dot_general(a, b, (((1,), (1,)), ((), ())), preferred_element_type=F32)


def _lam_init(layer_idx):
    return 0.8 - 0.6 * math.exp(-0.3 * layer_idx)


def _lam(lam_p, lam_init):
    return (jnp.exp(jnp.sum(lam_p[0:1] * lam_p[1:2], axis=1, keepdims=True))
            - jnp.exp(jnp.sum(lam_p[2:3] * lam_p[3:4], axis=1, keepdims=True)) + lam_init)


def _modulated_norm(x, g_pre, shift, scale):
    ms = jnp.mean(x * x, axis=-1, keepdims=True)
    return ((x * lax.rsqrt(ms + NORM_EPS)) * (g_pre * (1.0 + scale)) + shift).astype(BF16)


def _conv_taps(u_s, sl, sh_s, rows, wdw_ref, bdw):
    base = CONV_PAD - CONV_STATE
    n_sh = rows + CONV_PAD - SUBLANES
    for r in range(1, SUBLANES):
        sh_s[r - 1, 0:n_sh, :] = u_s[sl, pl.ds(r, n_sh), :]
    row_block = min(rows, 128)
    out_rows = []
    for rb in range(rows // row_block):
        cols = []
        for c in range(CONV_WIDTH // LANES):
            lanes = slice(c * LANES, (c + 1) * LANES)
            acc = jnp.broadcast_to(bdw[:, lanes], (row_block, LANES))
            for o in range(base, base + CONV_KERNEL):
                r, q8 = o % SUBLANES, o - o % SUBLANES
                start = rb * row_block + q8
                if r == 0:
                    win = u_s[sl, start:start + row_block, lanes]
                else:
                    win = sh_s[r - 1, start:start + row_block, lanes]
                acc = acc + wdw_ref[sl, o - base:o - base + 1, lanes] * win
            cols.append(acc)
        out_rows.append(jnp.concatenate(cols, axis=1))
    return jnp.concatenate(out_rows, axis=0)


def _conv_branch(hb, w_r, sl, prm, u_s, sh_s):
    rows = hb.shape[0]
    u = (_dot(hb, w_r[sl, :, WR_GLU_A:WR_GLU_A + CONV_WIDTH])
         * jax.nn.sigmoid(_dot(hb, w_r[sl, :, WR_GLU_B:WR_GLU_B + CONV_WIDTH])))
    u_s[sl, CONV_PAD:CONV_PAD + rows, :] = u
    conv = _conv_taps(u_s, sl, sh_s, rows, prm["w_dw"], prm["b_dw"][sl])
    mu = jnp.mean(conv, axis=-1, keepdims=True)
    xc = conv - mu
    ln = xc * lax.rsqrt(jnp.mean(xc * xc, axis=-1, keepdims=True) + LN_EPS) * prm["g_ln"][sl] + prm["b_ln"][sl]
    gc = _dot(hb, w_r[sl, :, WR_GC:WR_GC + CONV_WIDTH])
    return (_silu(ln) * _silu(gc)).astype(BF16)


def _out_proj(x, attn_out, conv_out, w_out, sl, g_post, gate):
    m = _dot(attn_out, w_out[sl, 0:ATTN_WIDTH, :]) + _dot(conv_out, w_out[sl, ATTN_WIDTH:ATTN_WIDTH + CONV_WIDTH, :])
    mn = m * lax.rsqrt(jnp.mean(m * m, axis=-1, keepdims=True) + NORM_EPS) * g_post
    return x + gate * mn


def _rope_t(zt, cos8, sin8):
    parts = []
    for mb in range(zt.shape[0] // QK_HEAD_DIM):
        b0 = mb * QK_HEAD_DIM
        x1, x2 = zt[b0:b0 + ROT_HALF], zt[b0 + ROT_HALF:b0 + ROT_DIM]
        parts += [x1 * cos8 - x2 * sin8, x2 * cos8 + x1 * sin8, zt[b0 + ROT_DIM:b0 + QK_HEAD_DIM]]
    return jnp.concatenate(parts, axis=0)


def _attention_t(q_s, k_own, vt_own, mask_t, n_past, k_s, vt_s, kl, att):
    acc_s, m_s, l_s, a_s, s_s, p_s = att
    def step(k_tile, vt_tile, first):
        for h in range(N_HEADS):
            hs = slice(h * V_HEAD_DIM, (h + 1) * V_HEAD_DIM)
            kh = k_tile(hs)
            for mp in range(2):
                s_s[2 * h + mp] = _dot(kh, q_s[mp, hs, :])
        for idx in range(2 * N_HEADS):
            st = s_s[idx]
            if first:
                st = jnp.where(mask_t, st, MASK_VALUE)
                mx = jnp.max(st, axis=0, keepdims=True)
                p = jnp.exp2(st - mx)
                l_s[idx] = jnp.sum(p, axis=0, keepdims=True)
                m_s[idx] = mx
            else:
                m_old = m_s[idx]
                m_new = jnp.maximum(m_old, jnp.max(st, axis=0, keepdims=True))
                alpha = jnp.exp2(m_old - m_new)
                p = jnp.exp2(st - m_new)
                l_s[idx] = alpha * l_s[idx] + jnp.sum(p, axis=0, keepdims=True)
                a_s[idx] = alpha
                m_s[idx] = m_new
            p_s[idx] = p.astype(BF16)
        for h in range(N_HEADS):
            vth = vt_tile(slice(h * V_HEAD_DIM, (h + 1) * V_HEAD_DIM))
            for mp in range(2):
                idx = 2 * h + mp
                pv = _dot(vth, p_s[idx])
                acc_s[idx] = pv if first else a_s[idx] * acc_s[idx] + pv

    step(lambda hs: k_own[:, hs], lambda hs: vt_own[hs, :], True)

    def body(j, carry):
        step(lambda hs: k_s[kl, j, :, hs], lambda hs: vt_s[kl, j, hs, :], False)
        return carry

    lax.fori_loop(0, n_past, body, 0)


def _prompt_layer(x, l, i, mod_ref, prm, cos8, sin8, mask_t, k_ref, v_ref, scr):
    k_s, vt_s, u_s, sh_s, q_s = scr[:5]
    acc_s, l_s = scr[5], scr[7]
    rows = x.shape[0]
    lam_init = _lam_init(l)
    shift, scale, gate = mod_ref[l, 0, 0:1, :], mod_ref[l, 0, 1:2, :], mod_ref[l, 0, 2:3, :]
    hb = _modulated_norm(x, prm["g_pre"][l], shift, scale)
    w_t, w_r = prm["w_t"], prm["w_r"]
    conv_out = _conv_branch(hb, w_r, l, prm, u_s, sh_s)

    qt = _rope_t(_dot_nt(w_t[l, WT_Q:WT_Q + QK_WIDTH, :], hb), cos8, sin8) * (QK_HEAD_DIM ** -0.5 * LOG2E)
    zero = jnp.zeros((QK_HEAD_DIM, rows), F32)
    q0, q1 = [], []
    for mb in range(QK_WIDTH // QK_HEAD_DIM):
        blk = qt[mb * QK_HEAD_DIM:(mb + 1) * QK_HEAD_DIM]
        q0.append(blk if mb % 2 == 0 else zero)
        q1.append(zero if mb % 2 == 0 else blk)
    q_s[0] = jnp.concatenate(q0, axis=0).astype(BF16)
    q_s[1] = jnp.concatenate(q1, axis=0).astype(BF16)

    kt = _rope_t(_dot_nt(w_t[l, WT_K:WT_K + QK_WIDTH, :], hb), cos8, sin8)
    k_ref[l, 0] = kt
    k_own = kt.T.astype(BF16)
    v = _dot(hb, w_r[l, :, WR_V:WR_V + ATTN_WIDTH])
    for h in range(N_HEADS):
        v_ref[l, 0, pl.ds(h, rows, stride=N_HEADS), :] = v[:, h * V_HEAD_DIM:(h + 1) * V_HEAD_DIM]
    vt_own = v.T.astype(BF16)

    _attention_t(q_s, k_own, vt_own, mask_t, i, k_s, vt_s, l, scr[5:])
    k_s[l, i] = k_own
    vt_s[l, i] = vt_own

    lam = _lam(prm["lam"][l], lam_init)
    gat = _dot_nt(w_t[l, WT_GA:WT_GA + ATTN_WIDTH, :], hb)
    g_sub = prm["g_subln_t"][l]
    heads = []
    for hd in range(N_HEADS):
        hs = slice(hd * V_HEAD_DIM, (hd + 1) * V_HEAD_DIM)
        o = acc_s[2 * hd] * (1.0 / l_s[2 * hd]) - acc_s[2 * hd + 1] * (lam / l_s[2 * hd + 1])
        o = o * lax.rsqrt(jnp.mean(o * o, axis=0, keepdims=True) + SUBLN_EPS) * g_sub
        heads.append((o * (1.0 - lam_init)) * _silu(gat[hs]))
    attn_out = jnp.concatenate(heads, axis=0).T.astype(BF16)
    return _out_proj(x, attn_out, conv_out, prm["w_out"], l, prm["g_post"][l], gate)


_PROMPT_PRM = ("g_pre", "g_post", "w_t", "w_r", "w_out", "lam", "g_subln_t", "w_dw", "b_dw", "g_ln", "b_ln")


def _prompt_kernel(depth, tile, x_ref, mod_ref, cos_ref, sin_ref, *rest):
    n_prm = len(_PROMPT_PRM)
    prm = dict(zip(_PROMPT_PRM, rest[:n_prm]))
    y_ref, k_ref, v_ref, conv_ref = rest[n_prm:n_prm + 4]
    scr = rest[n_prm + 4:]
    u_s = scr[2]
    i = pl.program_id(1)

    @pl.when(i == 0)
    def _():
        u_s[:, 0:CONV_PAD, :] = jnp.zeros((depth, CONV_PAD, CONV_WIDTH), F32)

    cos8, sin8 = cos_ref[...], sin_ref[...]
    shift = int(math.log2(CHUNK))
    k_chunk = lax.shift_right_logical(lax.broadcasted_iota(jnp.int32, (tile, tile), 0), shift)
    q_chunk = lax.shift_right_logical(lax.broadcasted_iota(jnp.int32, (tile, tile), 1), shift)
    mask_t = k_chunk <= q_chunk

    x = x_ref[0]
    for l in range(depth):
        x = _prompt_layer(x, l, i, mod_ref, prm, cos8, sin8, mask_t, k_ref, v_ref, scr)
        conv_ref[l, 0] = u_s[l, tile + CONV_PAD - CONV_STATE:tile + CONV_PAD, :]
        u_s[l, 0:CONV_PAD, :] = u_s[l, tile:tile + CONV_PAD, :]
    y_ref[0] = x


def _rope(z, cos_t, sa_t, sb_t):
    blocks = []
    for c in range(z.shape[1] // LANES):
        zc = z[:, c * LANES:(c + 1) * LANES]
        up = pltpu.roll(zc, LANES - ROT_HALF, 1)
        down = pltpu.roll(zc, ROT_HALF, 1)
        blocks.append(zc * cos_t + up * sa_t + down * sb_t)
    return jnp.concatenate(blocks, axis=1)


def _attention(q_s, k_own, v_own, n_past, kt_s, v_s, acc_s, m_s, l_s):
    for h in range(N_HEADS):
        hs = slice(h * V_HEAD_DIM, (h + 1) * V_HEAD_DIM)
        for mp in range(2):
            idx = 2 * h + mp
            s = _dot_nt(q_s[mp, :, hs], k_own[:, hs])
            mx = jnp.max(s, axis=1, keepdims=True)
            p = jnp.exp(s - mx)
            m_s[idx] = mx
            l_s[idx] = jnp.sum(p, axis=1, keepdims=True)
            acc_s[idx] = _dot(p.astype(BF16), v_own[:, hs])

    def body(j, carry):
        r0 = pl.multiple_of(j * KEY_TILE, KEY_TILE)
        for h in range(N_HEADS):
            hs = slice(h * V_HEAD_DIM, (h + 1) * V_HEAD_DIM)
            kth = kt_s[j, hs, :]
            vt = v_s[pl.ds(r0, KEY_TILE), hs]
            for mp in range(2):
                idx = 2 * h + mp
                s = _dot(q_s[mp, :, hs], kth)
                m_old = m_s[idx]
                m_new = jnp.maximum(m_old, jnp.max(s, axis=1, keepdims=True))
                alpha = jnp.exp(m_old - m_new)
                p = jnp.exp(s - m_new)
                l_s[idx] = alpha * l_s[idx] + jnp.sum(p, axis=1, keepdims=True)
                acc_s[idx] = alpha * acc_s[idx] + _dot(p.astype(BF16), vt)
                m_s[idx] = m_new
        return carry

    lax.fori_loop(0, n_past, body, 0)


_SAMPLE_PRM = ("g_pre", "g_post", "w_t", "w_r", "w_out", "lam", "g_subln", "w_dw", "b_dw", "g_ln", "b_ln")


def _sample_kernel(depth, rows, past_len, x_ref, mod_ref, cos_ref, sa_ref, sb_ref, ck_ref, cv_ref, cs_ref, *rest):
    n_prm = len(_SAMPLE_PRM)
    prm = dict(zip(_SAMPLE_PRM, rest[:n_prm]))
    y_ref, k_ref, v_ref, conv_ref = rest[n_prm:n_prm + 4]
    kt_s, v_s, u_s, sh_s, q_s, acc_s, m_s, l_s, xin_s, y1_s = rest[n_prm + 4:]
    l = pl.program_id(0)
    b = pl.program_id(1)

    @pl.when(l == 0)
    def _():
        xin_s[...] = x_ref[0]

    @pl.when(l != 0)
    def _():
        xin_s[...] = y1_s[b]

    for j in range(past_len // KEY_TILE):
        kt_s[j] = ck_ref[0, 0, :, j * KEY_TILE:(j + 1) * KEY_TILE].astype(BF16)
    for h in range(N_HEADS):
        v_s[:, h * V_HEAD_DIM:(h + 1) * V_HEAD_DIM] = cv_ref[0, 0, pl.ds(h, past_len, stride=N_HEADS), :].astype(BF16)
    u_s[0, 0:CONV_PAD, :] = jnp.zeros((CONV_PAD, CONV_WIDTH), F32)
    u_s[0, CONV_PAD - CONV_STATE:CONV_PAD, :] = cs_ref[0, 0]

    x = xin_s[...]
    lam_init = jnp.where(l == 0, _lam_init(0), _lam_init(1)).astype(F32)
    shift, scale, gate = mod_ref[0, 0, 0:1, :], mod_ref[0, 0, 1:2, :], mod_ref[0, 0, 2:3, :]
    hb = _modulated_norm(x, prm["g_pre"][0], shift, scale)
    w_t, w_r = prm["w_t"], prm["w_r"]
    tabs = (cos_ref[...], sa_ref[...], sb_ref[...])

    q = _rope(_dot_nt(hb, w_t[0, WT_Q:WT_Q + QK_WIDTH, :]), *tabs) * (QK_HEAD_DIM ** -0.5)
    first_map = (lax.broadcasted_iota(jnp.int32, (rows, QK_WIDTH), 1) & (V_HEAD_DIM - 1)) < QK_HEAD_DIM
    q_s[0] = jnp.where(first_map, q, 0.0).astype(BF16)
    q_s[1] = jnp.where(first_map, 0.0, q).astype(BF16)
    k = _rope(_dot_nt(hb, w_t[0, WT_K:WT_K + QK_WIDTH, :]), *tabs)
    v = _dot(hb, w_r[0, :, WR_V:WR_V + ATTN_WIDTH])
    k_ref[0, 0] = k
    v_ref[0, 0] = v
    _attention(q_s, k.astype(BF16), v.astype(BF16), past_len // KEY_TILE, kt_s, v_s, acc_s, m_s, l_s)

    lam = _lam(prm["lam"][0], lam_init)
    ga = _dot_nt(hb, w_t[0, WT_GA:WT_GA + ATTN_WIDTH, :])
    g_sub = prm["g_subln"][0]
    heads = []
    for hd in range(N_HEADS):
        hs = slice(hd * V_HEAD_DIM, (hd + 1) * V_HEAD_DIM)
        o = acc_s[2 * hd] * (1.0 / l_s[2 * hd]) - acc_s[2 * hd + 1] * (lam / l_s[2 * hd + 1])
        o = o * lax.rsqrt(jnp.mean(o * o, axis=-1, keepdims=True) + SUBLN_EPS) * g_sub
        heads.append(((o * (1.0 - lam_init)) * _silu(ga[:, hs])).astype(BF16))
    attn_out = jnp.concatenate(heads, axis=1)

    conv_out = _conv_branch(hb, w_r, 0, prm, u_s, sh_s)
    y = _out_proj(x, attn_out, conv_out, prm["w_out"], 0, prm["g_post"][0], gate)
    conv_ref[0, 0] = u_s[0, rows + CONV_PAD - CONV_STATE:rows + CONV_PAD, :]
    y1_s[b] = y

    @pl.when(l == depth - 1)
    def _():
        y_ref[0] = y


def _mod_kernel(c_ref, w_ref, b_ref, o_ref):
    c = c_ref[...]
    o_ref[0] = _dot(_silu(c), w_ref[0]) + b_ref[0]


def _rope_angles(pos):
    inv = jnp.float32(ROPE_THETA) ** (-jnp.arange(0, ROT_DIM, 2, dtype=F32) / ROT_DIM)
    return pos.astype(F32)[:, None] * inv[None, :]


def _rope_tables_rowmajor(pos):
    ang = _rope_angles(pos)
    lane = jnp.arange(LANES, dtype=jnp.int32) % QK_HEAD_DIM
    ang = ang[:, lane % ROT_HALF]
    cos, sin = jnp.cos(ang), jnp.sin(ang)
    lo = (lane < ROT_HALF)[None, :]
    hi = ((lane >= ROT_HALF) & (lane < ROT_DIM))[None, :]
    return (jnp.where(lo | hi, cos, 1.0), jnp.where(lo, -sin, 0.0), jnp.where(hi, sin, 0.0))


def _const_spec(shape):
    nd = len(shape)
    return pl.BlockSpec(shape, lambda *_: (0,) * nd, pipeline_mode=pl.Buffered(1))


def kernel(x_prompt, x_sample, c_prompt, c_sample, cache_k, cache_v, state_conv, w_ada, b_ada, g_pre, g_post, w_in,
           w_out, lam_q1, lam_k1, lam_q2, lam_k2, g_subln, w_dw, b_dw, g_ln, b_ln):
    batch, seq, _ = x_prompt.shape
    dec_batch, dec_seq, _ = x_sample.shape
    depth = w_in.shape[0]
    past_len = cache_k.shape[2]
    tile = min(PROMPT_TILE, seq)
    n_tiles = seq // tile
    assert seq % tile == 0 and tile % LANES == 0 and past_len % KEY_TILE == 0 and dec_seq % SUBLANES == 0

    c_all = jnp.concatenate([c_prompt, c_sample], axis=0)
    n_c = c_all.shape[0]
    mod_cols = 512
    mod = pl.pallas_call(
        _mod_kernel,
        grid=(depth, 3 * D_MODEL // mod_cols),
        in_specs=[pl.BlockSpec((n_c, D_MODEL), lambda l, n: (0, 0)),
                  pl.BlockSpec((1, D_MODEL, mod_cols), lambda l, n: (l, 0, n)),
                  pl.BlockSpec((1, 1, mod_cols), lambda l, n: (l, 0, n))],
        out_specs=pl.BlockSpec((1, n_c, mod_cols), lambda l, n: (l, 0, n)),
        out_shape=jax.ShapeDtypeStruct((depth, n_c, 3 * D_MODEL), F32),
        name="adaln_mod",
    )(c_all, w_ada, b_ada.reshape(depth, 1, 3 * D_MODEL))
    mod_p = mod[:, :batch].reshape(depth, batch, 3, D_MODEL)
    mod_s = mod[:, batch:].reshape(depth, dec_batch, 3, D_MODEL)

    w_in_b = w_in.astype(BF16)
    w_t = jnp.swapaxes(jnp.concatenate([w_in_b[:, :, COL_Q:COL_V], w_in_b[:, :, COL_GA:COL_GLU_A]], axis=2), 1, 2)
    w_r = jnp.concatenate([w_in_b[:, :, COL_V:COL_GA], w_in_b[:, :, COL_GLU_A:IN_COLS]], axis=2)
    w_dw_p = jnp.concatenate([w_dw, jnp.zeros((depth, CONV_PAD - CONV_KERNEL, CONV_WIDTH), w_dw.dtype)], axis=1)
    shared = {
        "g_pre": g_pre.reshape(depth, 1, D_MODEL), "g_post": g_post.reshape(depth, 1, D_MODEL),
        "w_t": w_t, "w_r": w_r, "w_out": w_out.astype(BF16),
        "lam": jnp.stack([lam_q1, lam_k1, lam_q2, lam_k2], axis=1),
        "g_subln": g_subln.reshape(depth, 1, V_HEAD_DIM), "g_subln_t": g_subln.reshape(depth, V_HEAD_DIM, 1),
        "w_dw": w_dw_p, "b_dw": b_dw.reshape(depth, 1, CONV_WIDTH),
        "g_ln": g_ln.reshape(depth, 1, CONV_WIDTH), "b_ln": b_ln.reshape(depth, 1, CONV_WIDTH)}

    ang = _rope_angles(jnp.arange(seq, dtype=jnp.int32)).T
    params_p = [shared[n] for n in _PROMPT_PRM]
    tab_spec = pl.BlockSpec((ROT_HALF, tile), lambda b, i: (0, i))
    yp, kp_t, vp, cp = pl.pallas_call(
        functools.partial(_prompt_kernel, depth, tile),
        grid=(batch, n_tiles),
        in_specs=[pl.BlockSpec((1, tile, D_MODEL), lambda b, i: (b, i, 0)),
                  pl.BlockSpec((depth, 1, 3, D_MODEL), lambda b, i: (0, b, 0, 0)),
                  tab_spec, tab_spec]
                 + [_const_spec(p.shape) for p in params_p],
        out_specs=[pl.BlockSpec((1, tile, D_MODEL), lambda b, i: (b, i, 0)),
                   pl.BlockSpec((depth, 1, QK_WIDTH, tile), lambda b, i: (0, b, 0, i)),
                   pl.BlockSpec((depth, 1, tile * N_HEADS, V_HEAD_DIM), lambda b, i: (0, b, i, 0)),
                   pl.BlockSpec((depth, 1, CONV_STATE, CONV_WIDTH), lambda b, i: (0, b, 0, 0))],
        out_shape=[jax.ShapeDtypeStruct((batch, seq, D_MODEL), F32),
                   jax.ShapeDtypeStruct((depth, batch, QK_WIDTH, seq), F32),
                   jax.ShapeDtypeStruct((depth, batch, seq * N_HEADS, V_HEAD_DIM), F32),
                   jax.ShapeDtypeStruct((depth, batch, CONV_STATE, CONV_WIDTH), F32)],
        scratch_shapes=[pltpu.VMEM((depth, n_tiles, tile, QK_WIDTH), BF16),
                        pltpu.VMEM((depth, n_tiles, ATTN_WIDTH, tile), BF16),
                        pltpu.VMEM((depth, CONV_PAD + tile, CONV_WIDTH), F32),
                        pltpu.VMEM((SUBLANES - 1, CONV_PAD + tile, CONV_WIDTH), F32),
                        pltpu.VMEM((2, QK_WIDTH, tile), BF16),
                        pltpu.VMEM((2 * N_HEADS, V_HEAD_DIM, tile), F32),
                        pltpu.VMEM((2 * N_HEADS, 1, tile), F32), pltpu.VMEM((2 * N_HEADS, 1, tile), F32),
                        pltpu.VMEM((2 * N_HEADS, 1, tile), F32),
                        pltpu.VMEM((2 * N_HEADS, tile, tile), F32), pltpu.VMEM((2 * N_HEADS, tile, tile), BF16)],
        compiler_params=pltpu.CompilerParams(dimension_semantics=("arbitrary", "arbitrary"),
                                             vmem_limit_bytes=VMEM_LIMIT_BYTES),
        name="prompt_layers",
    )(x_prompt, mod_p, jnp.cos(ang), jnp.sin(ang), *params_p)
    kp = jnp.transpose(kp_t.reshape(depth, batch, N_HEADS, 2, QK_HEAD_DIM, seq), (0, 1, 5, 2, 3, 4))

    tabs_s = _rope_tables_rowmajor(past_len + jnp.arange(dec_seq, dtype=jnp.int32))
    params_s = [shared[n] for n in _SAMPLE_PRM]

    def layer_spec(shape):
        nd = len(shape)
        return pl.BlockSpec((1,) + tuple(shape[1:]), lambda l, b: (l,) + (0,) * (nd - 1))

    lb_spec = lambda shape: pl.BlockSpec((1, 1) + tuple(shape[2:]), lambda l, b: (l, b) + (0,) * (len(shape) - 2))
    ck_t = jnp.transpose(cache_k, (0, 1, 3, 4, 5, 2)).reshape(depth, dec_batch, QK_WIDTH, past_len)
    cv = cache_v.reshape(depth, dec_batch, past_len * N_HEADS, V_HEAD_DIM)
    row_spec = pl.BlockSpec((1, dec_seq, D_MODEL), lambda l, b: (b, 0, 0))
    stab_spec = pl.BlockSpec((dec_seq, LANES), lambda l, b: (0, 0))
    ys, ks, vs, cs = pl.pallas_call(
        functools.partial(_sample_kernel, depth, dec_seq, past_len),
        grid=(depth, dec_batch),
        in_specs=[row_spec, lb_spec(mod_s.shape), stab_spec, stab_spec, stab_spec,
                  lb_spec(ck_t.shape), lb_spec(cv.shape), lb_spec(state_conv.shape)]
                 + [layer_spec(p.shape) for p in params_s],
        out_specs=[pl.BlockSpec((1, dec_seq, D_MODEL), lambda l, b: (jnp.where(l == depth - 1, b, 0), 0, 0)),
                   lb_spec((depth, dec_batch, dec_seq, QK_WIDTH)),
                   lb_spec((depth, dec_batch, dec_seq, ATTN_WIDTH)), lb_spec(state_conv.shape)],
        out_shape=[jax.ShapeDtypeStruct((dec_batch, dec_seq, D_MODEL), F32),
                   jax.ShapeDtypeStruct((depth, dec_batch, dec_seq, QK_WIDTH), F32),
                   jax.ShapeDtypeStruct((depth, dec_batch, dec_seq, ATTN_WIDTH), F32),
                   jax.ShapeDtypeStruct(state_conv.shape, F32)],
        scratch_shapes=[pltpu.VMEM((past_len // KEY_TILE, QK_WIDTH, KEY_TILE), BF16),
                        pltpu.VMEM((past_len, ATTN_WIDTH), BF16),
                        pltpu.VMEM((1, CONV_PAD + dec_seq, CONV_WIDTH), F32),
                        pltpu.VMEM((SUBLANES - 1, CONV_PAD + dec_seq, CONV_WIDTH), F32),
                        pltpu.VMEM((2, dec_seq, QK_WIDTH), BF16),
                        pltpu.VMEM((2 * N_HEADS, dec_seq, V_HEAD_DIM), F32),
                        pltpu.VMEM((2 * N_HEADS, dec_seq, 1), F32), pltpu.VMEM((2 * N_HEADS, dec_seq, 1), F32),
                        pltpu.VMEM((dec_seq, D_MODEL), F32), pltpu.VMEM((dec_batch, dec_seq, D_MODEL), F32)],
        compiler_params=pltpu.CompilerParams(dimension_semantics=("arbitrary", "arbitrary"),
                                             vmem_limit_bytes=VMEM_LIMIT_BYTES),
        name="sample_layers",
    )(x_sample, mod_s, *tabs_s, ck_t, cv, state_conv, *params_s)

    return (yp, ys, kp, vp.reshape(depth, batch, seq, N_HEADS, V_HEAD_DIM), cp,
            ks.reshape(depth, dec_batch, dec_seq, N_HEADS, 2, QK_HEAD_DIM),
            vs.reshape(depth, dec_batch, dec_seq, N_HEADS, V_HEAD_DIM), cs)
```

```python
import functools
import math

import jax
import jax.numpy as jnp
from jax import lax
from jax.experimental import pallas as pl
from jax.experimental.pallas import tpu as pltpu

F32 = jnp.float32
BF16 = jnp.bfloat16

D_MODEL = 1024
N_HEADS = 4
QK_HEAD_DIM = 64
V_HEAD_DIM = 128
ATTN_WIDTH = N_HEADS * V_HEAD_DIM
QK_WIDTH = N_HEADS * 2 * QK_HEAD_DIM
CONV_WIDTH = D_MODEL - ATTN_WIDTH
CONV_KERNEL = 31
CONV_STATE = CONV_KERNEL - 1
CHUNK = 64
ROT_DIM = 16
ROT_HALF = ROT_DIM // 2
ROPE_THETA = 500000.0
NORM_EPS = 1e-6
SUBLN_EPS = 1e-5
LN_EPS = 1e-5
COL_Q, COL_K, COL_V, COL_GA = 0, QK_WIDTH, 2 * QK_WIDTH, 2 * QK_WIDTH + ATTN_WIDTH
COL_GLU_A = 2 * QK_WIDTH + 2 * ATTN_WIDTH
COL_GLU_B = COL_GLU_A + CONV_WIDTH
COL_GC = COL_GLU_B + CONV_WIDTH
IN_COLS = COL_GC + CONV_WIDTH
WT_Q, WT_K, WT_GA = 0, QK_WIDTH, 2 * QK_WIDTH
WR_V, WR_GLU_A, WR_GLU_B, WR_GC = 0, ATTN_WIDTH, ATTN_WIDTH + CONV_WIDTH, ATTN_WIDTH + 2 * CONV_WIDTH

LANES = 128
SUBLANES = 8
CONV_PAD = 32
MASK_VALUE = -1e30
LOG2E = math.log2(math.e)
PROMPT_TILE = 256
KEY_TILE = 256
VMEM_LIMIT_BYTES = 60000 * 1024
VT_HEAD_ROWS = V_HEAD_DIM + 16


def _silu(x):
    return x * jax.nn.sigmoid(x)


def _dot(a, b):
    return jnp.dot(a, b, preferred_element_type=F32)


def _dot_nt(a, b):
    return lax.dot_general(a, b, (((1,), (1,)), ((), ())), preferred_element_type=F32)


def _lam_init(layer_idx):
    return 0.8 - 0.6 * math.exp(-0.3 * layer_idx)


def _lam(lam_p, lam_init):
    return (jnp.exp(jnp.sum(lam_p[0:1] * lam_p[1:2], axis=1, keepdims=True))
            - jnp.exp(jnp.sum(lam_p[2:3] * lam_p[3:4], axis=1, keepdims=True)) + lam_init)


def _modulated_norm(x, g_pre, shift, scale):
    ms = jnp.mean(x * x, axis=-1, keepdims=True)
    return ((x * lax.rsqrt(ms + NORM_EPS)) * (g_pre * (1.0 + scale)) + shift).astype(BF16)


def _conv_taps(u_s, sl, sh_s, rows, wdw_ref, bdw):
    base = CONV_PAD - CONV_STATE
    n_sh = rows + CONV_PAD - SUBLANES
    for r in range(1, SUBLANES):
        sh_s[r - 1, 0:n_sh, :] = u_s[sl, pl.ds(r, n_sh), :]
    row_block = min(rows, 128)
    out_rows = []
    for rb in range(rows // row_block):
        cols = []
        for c in range(CONV_WIDTH // LANES):
            lanes = slice(c * LANES, (c + 1) * LANES)
            acc = jnp.broadcast_to(bdw[:, lanes], (row_block, LANES))
            for o in range(base, base + CONV_KERNEL):
                r, q8 = o % SUBLANES, o - o % SUBLANES
                start = rb * row_block + q8
                if r == 0:
                    win = u_s[sl, start:start + row_block, lanes]
                else:
                    win = sh_s[r - 1, start:start + row_block, lanes]
                acc = acc + wdw_ref[sl, o - base:o - base + 1, lanes] * win
            cols.append(acc)
        out_rows.append(jnp.concatenate(cols, axis=1))
    return jnp.concatenate(out_rows, axis=0)


def _conv_branch(hb, w_r, sl, prm, u_s, sh_s):
    rows = hb.shape[0]
    u = (_dot(hb, w_r[sl, :, WR_GLU_A:WR_GLU_A + CONV_WIDTH])
         * jax.nn.sigmoid(_dot(hb, w_r[sl, :, WR_GLU_B:WR_GLU_B + CONV_WIDTH])))
    u_s[sl, CONV_PAD:CONV_PAD + rows, :] = u
    conv = _conv_taps(u_s, sl, sh_s, rows, prm["w_dw"], prm["b_dw"][sl])
    mu = jnp.mean(conv, axis=-1, keepdims=True)
    xc = conv - mu
    ln = xc * lax.rsqrt(jnp.mean(xc * xc, axis=-1, keepdims=True) + LN_EPS) * prm["g_ln"][sl] + prm["b_ln"][sl]
    gc = _dot(hb, w_r[sl, :, WR_GC:WR_GC + CONV_WIDTH])
    return (_silu(ln) * _silu(gc)).astype(BF16)


def _out_proj(x, attn_out, conv_out, w_out, sl, g_post, gate):
    m = _dot(attn_out, w_out[sl, 0:ATTN_WIDTH, :]) + _dot(conv_out, w_out[sl, ATTN_WIDTH:ATTN_WIDTH + CONV_WIDTH, :])
    mn = m * lax.rsqrt(jnp.mean(m * m, axis=-1, keepdims=True) + NORM_EPS) * g_post
    return x + gate * mn


def _rope_t(zt, cos8, sin8):
    parts = []
    for mb in range(zt.shape[0] // QK_HEAD_DIM):
        b0 = mb * QK_HEAD_DIM
        x1, x2 = zt[b0:b0 + ROT_HALF], zt[b0 + ROT_HALF:b0 + ROT_DIM]
        parts += [x1 * cos8 - x2 * sin8, x2 * cos8 + x1 * sin8, zt[b0 + ROT_DIM:b0 + QK_HEAD_DIM]]
    return jnp.concatenate(parts, axis=0)


def _attention_t(q_s, tile_idx, k_s, vt_s, kl, att):
    acc_s, m_s, a_s, s_s, p_s = att
    tile = p_s.shape[1]
    chunk_shift = int(math.log2(CHUNK))

    def scores(j, buf):
        for h in range(N_HEADS):
            hs = slice(h * V_HEAD_DIM, (h + 1) * V_HEAD_DIM)
            kh = k_s[kl, j, :, hs]
            for mp in range(2):
                s_s[buf, 2 * h + mp] = _dot(kh, q_s[mp, hs, :])

    def softmax(first, buf):
        for idx in range(2 * N_HEADS):
            st = s_s[buf, idx]
            if first:
                k_chunk = lax.shift_right_logical(lax.broadcasted_iota(jnp.int32, (tile, tile), 0), chunk_shift)
                q_chunk = lax.shift_right_logical(lax.broadcasted_iota(jnp.int32, (tile, tile), 1), chunk_shift)
                st = jnp.where(k_chunk <= q_chunk, st, MASK_VALUE)
                m_new = jnp.max(st, axis=0, keepdims=True)
                a_s[idx] = jnp.ones((1, tile), F32)
            else:
                m_old = m_s[idx]
                m_new = jnp.maximum(m_old, jnp.max(st, axis=0, keepdims=True))
                a_s[idx] = jnp.exp2(m_old - m_new)
            m_s[idx] = m_new
            p_s[idx] = jnp.exp2(st - m_new).astype(BF16)

    def numerators(j):
        for h in range(N_HEADS):
            vth = vt_s[kl, j, h * VT_HEAD_ROWS:(h + 1) * VT_HEAD_ROWS, :]
            for mp in range(2):
                idx = 2 * h + mp
                acc_s[idx] = a_s[idx] * acc_s[idx] + _dot(vth, p_s[idx])

    last_past = jnp.maximum(tile_idx - 1, 0)
    acc_s[...] = jnp.zeros(acc_s.shape, F32)
    scores(tile_idx, 0)
    scores(0, 1)
    softmax(True, 0)

    def stage(j, buf):
        numerators(jnp.where(j == 0, tile_idx, j - 1))
        scores(jnp.minimum(j + 1, last_past), buf)
        softmax(False, 1 - buf)

    def body(jj, carry):
        stage(2 * jj, 0)
        stage(2 * jj + 1, 1)
        return carry

    lax.fori_loop(0, lax.shift_right_logical(tile_idx, 1), body, 0)

    @pl.when((tile_idx & 1) == 1)
    def _():
        stage(tile_idx - 1, 0)

    numerators(jnp.where(tile_idx == 0, tile_idx, last_past))


def _prompt_layer(x, l, i, mod_ref, prm, cos8, sin8, k_ref, v_ref, scr):
    k_s, vt_s, u_s, sh_s, q_s, acc_s = scr[:6]
    rows = x.shape[0]
    lam_init = _lam_init(l)
    shift, scale, gate = mod_ref[l, 0, 0:1, :], mod_ref[l, 0, 1:2, :], mod_ref[l, 0, 2:3, :]
    hb = _modulated_norm(x, prm["g_pre"][l], shift, scale)
    w_t, w_r = prm["w_t"], prm["w_r"]
    conv_out = _conv_branch(hb, w_r, l, prm, u_s, sh_s)

    qt = _rope_t(_dot_nt(w_t[l, WT_Q:WT_Q + QK_WIDTH, :], hb), cos8, sin8) * (QK_HEAD_DIM ** -0.5 * LOG2E)
    zero = jnp.zeros((QK_HEAD_DIM, rows), F32)
    q0, q1 = [], []
    for mb in range(QK_WIDTH // QK_HEAD_DIM):
        blk = qt[mb * QK_HEAD_DIM:(mb + 1) * QK_HEAD_DIM]
        q0.append(blk if mb % 2 == 0 else zero)
        q1.append(zero if mb % 2 == 0 else blk)
    q_s[0] = jnp.concatenate(q0, axis=0).astype(BF16)
    q_s[1] = jnp.concatenate(q1, axis=0).astype(BF16)

    kt = _rope_t(_dot_nt(w_t[l, WT_K:WT_K + QK_WIDTH, :], hb), cos8, sin8)
    k_ref[l, 0] = kt
    k_own = kt.T.astype(BF16)
    v = _dot(hb, w_r[l, :, WR_V:WR_V + ATTN_WIDTH])
    for h in range(N_HEADS):
        v_ref[l, 0, pl.ds(h, rows, stride=N_HEADS), :] = v[:, h * V_HEAD_DIM:(h + 1) * V_HEAD_DIM]
    vt = v.T
    ones = jnp.ones((VT_HEAD_ROWS - V_HEAD_DIM, rows), F32)
    vt_s[l, i] = jnp.concatenate(
        [blk for h in range(N_HEADS) for blk in (vt[h * V_HEAD_DIM:(h + 1) * V_HEAD_DIM], ones)], axis=0).astype(BF16)
    k_s[l, i] = k_own
    _attention_t(q_s, i, k_s, vt_s, l, scr[5:])

    lam = _lam(prm["lam"][l], lam_init)
    gat = _dot_nt(w_t[l, WT_GA:WT_GA + ATTN_WIDTH, :], hb)
    g_sub = prm["g_subln_t"][l]
    heads = []
    for hd in range(N_HEADS):
        hs = slice(hd * V_HEAD_DIM, (hd + 1) * V_HEAD_DIM)
        n0, n1 = acc_s[2 * hd], acc_s[2 * hd + 1]
        l0, l1 = n0[V_HEAD_DIM:V_HEAD_DIM + 1], n1[V_HEAD_DIM:V_HEAD_DIM + 1]
        o = n0[0:V_HEAD_DIM] * (1.0 / l0) - n1[0:V_HEAD_DIM] * (lam / l1)
        o = o * lax.rsqrt(jnp.mean(o * o, axis=0, keepdims=True) + SUBLN_EPS) * g_sub
        heads.append((o * (1.0 - lam_init)) * _silu(gat[hs]))
    attn_out = jnp.concatenate(heads, axis=0).T.astype(BF16)
    return _out_proj(x, attn_out, conv_out, prm["w_out"], l, prm["g_post"][l], gate)


_PROMPT_PRM = ("g_pre", "g_post", "w_t", "w_r", "w_out", "lam", "g_subln_t", "w_dw", "b_dw", "g_ln", "b_ln")


def _prompt_kernel(depth, tile, x_ref, mod_ref, cos_ref, sin_ref, *rest):
    n_prm = len(_PROMPT_PRM)
    prm = dict(zip(_PROMPT_PRM, rest[:n_prm]))
    y_ref, k_ref, v_ref, conv_ref = rest[n_prm:n_prm + 4]
    scr = rest[n_prm + 4:]
    u_s = scr[2]
    i = pl.program_id(1)

    @pl.when(i == 0)
    def _():
        u_s[:, 0:CONV_PAD, :] = jnp.zeros((depth, CONV_PAD, CONV_WIDTH), F32)

    cos8, sin8 = cos_ref[...], sin_ref[...]
    x = x_ref[0]
    for l in range(depth):
        x = _prompt_layer(x, l, i, mod_ref, prm, cos8, sin8, k_ref, v_ref, scr)
        conv_ref[l, 0] = u_s[l, tile + CONV_PAD - CONV_STATE:tile + CONV_PAD, :]
        u_s[l, 0:CONV_PAD, :] = u_s[l, tile:tile + CONV_PAD, :]
    y_ref[0] = x


def _rope(z, cos_t, sa_t, sb_t):
    blocks = []
    for c in range(z.shape[1] // LANES):
        zc = z[:, c * LANES:(c + 1) * LANES]
        up = pltpu.roll(zc, LANES - ROT_HALF, 1)
        down = pltpu.roll(zc, ROT_HALF, 1)
        blocks.append(zc * cos_t + up * sa_t + down * sb_t)
    return jnp.concatenate(blocks, axis=1)


def _attention(q_s, k_own, v_own, n_past, kt_s, v_s, acc_s, m_s, l_s):
    for h in range(N_HEADS):
        hs = slice(h * V_HEAD_DIM, (h + 1) * V_HEAD_DIM)
        for mp in range(2):
            idx = 2 * h + mp
            s = _dot_nt(q_s[mp, :, hs], k_own[:, hs])
            mx = jnp.max(s, axis=1, keepdims=True)
            p = jnp.exp(s - mx)
            m_s[idx] = mx
            l_s[idx] = jnp.sum(p, axis=1, keepdims=True)
            acc_s[idx] = _dot(p.astype(BF16), v_own[:, hs])

    def body(j, carry):
        r0 = pl.multiple_of(j * KEY_TILE, KEY_TILE)
        for h in range(N_HEADS):
            hs = slice(h * V_HEAD_DIM, (h + 1) * V_HEAD_DIM)
            kth = kt_s[j, hs, :]
            vt = v_s[pl.ds(r0, KEY_TILE), hs]
            for mp in range(2):
                idx = 2 * h + mp
                s = _dot(q_s[mp, :, hs], kth)
                m_old = m_s[idx]
                m_new = jnp.maximum(m_old, jnp.max(s, axis=1, keepdims=True))
                alpha = jnp.exp(m_old - m_new)
                p = jnp.exp(s - m_new)
                l_s[idx] = alpha * l_s[idx] + jnp.sum(p, axis=1, keepdims=True)
                acc_s[idx] = alpha * acc_s[idx] + _dot(p.astype(BF16), vt)
                m_s[idx] = m_new
        return carry

    lax.fori_loop(0, n_past, body, 0)


_SAMPLE_PRM = ("g_pre", "g_post", "w_t", "w_r", "w_out", "lam", "g_subln", "w_dw", "b_dw", "g_ln", "b_ln")


def _sample_kernel(depth, rows, past_len, x_ref, mod_ref, cos_ref, sa_ref, sb_ref, ck_ref, cv_ref, cs_ref, *rest):
    n_prm = len(_SAMPLE_PRM)
    prm = dict(zip(_SAMPLE_PRM, rest[:n_prm]))
    y_ref, k_ref, v_ref, conv_ref = rest[n_prm:n_prm + 4]
    kt_s, v_s, u_s, sh_s, q_s, acc_s, m_s, l_s, xin_s, y1_s = rest[n_prm + 4:]
    l = pl.program_id(0)
    b = pl.program_id(1)

    @pl.when(l == 0)
    def _():
        xin_s[...] = x_ref[0]

    @pl.when(l != 0)
    def _():
        xin_s[...] = y1_s[b]

    for j in range(past_len // KEY_TILE):
        kt_s[j] = ck_ref[0, 0, :, j * KEY_TILE:(j + 1) * KEY_TILE].astype(BF16)
    for h in range(N_HEADS):
        v_s[:, h * V_HEAD_DIM:(h + 1) * V_HEAD_DIM] = cv_ref[0, 0, pl.ds(h, past_len, stride=N_HEADS), :].astype(BF16)
    u_s[0, 0:CONV_PAD, :] = jnp.zeros((CONV_PAD, CONV_WIDTH), F32)
    u_s[0, CONV_PAD - CONV_STATE:CONV_PAD, :] = cs_ref[0, 0]

    x = xin_s[...]
    lam_init = jnp.where(l == 0, _lam_init(0), _lam_init(1)).astype(F32)
    shift, scale, gate = mod_ref[0, 0, 0:1, :], mod_ref[0, 0, 1:2, :], mod_ref[0, 0, 2:3, :]
    hb = _modulated_norm(x, prm["g_pre"][0], shift, scale)
    w_t, w_r = prm["w_t"], prm["w_r"]
    tabs = (cos_ref[...], sa_ref[...], sb_ref[...])

    q = _rope(_dot_nt(hb, w_t[0, WT_Q:WT_Q + QK_WIDTH, :]), *tabs) * (QK_HEAD_DIM ** -0.5)
    first_map = (lax.broadcasted_iota(jnp.int32, (rows, QK_WIDTH), 1) & (V_HEAD_DIM - 1)) < QK_HEAD_DIM
    q_s[0] = jnp.where(first_map, q, 0.0).astype(BF16)
    q_s[1] = jnp.where(first_map, 0.0, q).astype(BF16)
    k = _rope(_dot_nt(hb, w_t[0, WT_K:WT_K + QK_WIDTH, :]), *tabs)
    v = _dot(hb, w_r[0, :, WR_V:WR_V + ATTN_WIDTH])
    k_ref[0, 0] = k
    v_ref[0, 0] = v
    _attention(q_s, k.astype(BF16), v.astype(BF16), past_len // KEY_TILE, kt_s, v_s, acc_s, m_s, l_s)

    lam = _lam(prm["lam"][0], lam_init)
    ga = _dot_nt(hb, w_t[0, WT_GA:WT_GA + ATTN_WIDTH, :])
    g_sub = prm["g_subln"][0]
    heads = []
    for hd in range(N_HEADS):
        hs = slice(hd * V_HEAD_DIM, (hd + 1) * V_HEAD_DIM)
        o = acc_s[2 * hd] * (1.0 / l_s[2 * hd]) - acc_s[2 * hd + 1] * (lam / l_s[2 * hd + 1])
        o = o * lax.rsqrt(jnp.mean(o * o, axis=-1, keepdims=True) + SUBLN_EPS) * g_sub
        heads.append(((o * (1.0 - lam_init)) * _silu(ga[:, hs])).astype(BF16))
    attn_out = jnp.concatenate(heads, axis=1)

    conv_out = _conv_branch(hb, w_r, 0, prm, u_s, sh_s)
    y = _out_proj(x, attn_out, conv_out, prm["w_out"], 0, prm["g_post"][0], gate)
    conv_ref[0, 0] = u_s[0, rows + CONV_PAD - CONV_STATE:rows + CONV_PAD, :]
    y1_s[b] = y

    @pl.when(l == depth - 1)
    def _():
        y_ref[0] = y


def _mod_kernel(c_ref, w_ref, b_ref, o_ref):
    c = c_ref[...]
    o_ref[0] = _dot(_silu(c), w_ref[0]) + b_ref[0]


def _rope_angles(pos):
    inv = jnp.float32(ROPE_THETA) ** (-jnp.arange(0, ROT_DIM, 2, dtype=F32) / ROT_DIM)
    return pos.astype(F32)[:, None] * inv[None, :]


def _rope_tables_rowmajor(pos):
    ang = _rope_angles(pos)
    lane = jnp.arange(LANES, dtype=jnp.int32) % QK_HEAD_DIM
    ang = ang[:, lane % ROT_HALF]
    cos, sin = jnp.cos(ang), jnp.sin(ang)
    lo = (lane < ROT_HALF)[None, :]
    hi = ((lane >= ROT_HALF) & (lane < ROT_DIM))[None, :]
    return (jnp.where(lo | hi, cos, 1.0), jnp.where(lo, -sin, 0.0), jnp.where(hi, sin, 0.0))


def _const_spec(shape):
    nd = len(shape)
    return pl.BlockSpec(shape, lambda *_: (0,) * nd, pipeline_mode=pl.Buffered(1))


def kernel(x_prompt, x_sample, c_prompt, c_sample, cache_k, cache_v, state_conv, w_ada, b_ada, g_pre, g_post, w_in,
           w_out, lam_q1, lam_k1, lam_q2, lam_k2, g_subln, w_dw, b_dw, g_ln, b_ln):
    batch, seq, _ = x_prompt.shape
    dec_batch, dec_seq, _ = x_sample.shape
    depth = w_in.shape[0]
    past_len = cache_k.shape[2]
    tile = min(PROMPT_TILE, seq)
    n_tiles = seq // tile
    assert seq % tile == 0 and tile % LANES == 0 and past_len % KEY_TILE == 0 and dec_seq % SUBLANES == 0

    c_all = jnp.concatenate([c_prompt, c_sample], axis=0)
    n_c = c_all.shape[0]
    mod_cols = 512
    mod = pl.pallas_call(
        _mod_kernel,
        grid=(depth, 3 * D_MODEL // mod_cols),
        in_specs=[pl.BlockSpec((n_c, D_MODEL), lambda l, n: (0, 0)),
                  pl.BlockSpec((1, D_MODEL, mod_cols), lambda l, n: (l, 0, n)),
                  pl.BlockSpec((1, 1, mod_cols), lambda l, n: (l, 0, n))],
        out_specs=pl.BlockSpec((1, n_c, mod_cols), lambda l, n: (l, 0, n)),
        out_shape=jax.ShapeDtypeStruct((depth, n_c, 3 * D_MODEL), F32),
        name="adaln_mod",
    )(c_all, w_ada, b_ada.reshape(depth, 1, 3 * D_MODEL))
    mod_p = mod[:, :batch].reshape(depth, batch, 3, D_MODEL)
    mod_s = mod[:, batch:].reshape(depth, dec_batch, 3, D_MODEL)

    w_in_b = w_in.astype(BF16)
    w_t = jnp.swapaxes(jnp.concatenate([w_in_b[:, :, COL_Q:COL_V], w_in_b[:, :, COL_GA:COL_GLU_A]], axis=2), 1, 2)
    w_r = jnp.concatenate([w_in_b[:, :, COL_V:COL_GA], w_in_b[:, :, COL_GLU_A:IN_COLS]], axis=2)
    w_dw_p = jnp.concatenate([w_dw, jnp.zeros((depth, CONV_PAD - CONV_KERNEL, CONV_WIDTH), w_dw.dtype)], axis=1)
    shared = {
        "g_pre": g_pre.reshape(depth, 1, D_MODEL), "g_post": g_post.reshape(depth, 1, D_MODEL),
        "w_t": w_t, "w_r": w_r, "w_out": w_out.astype(BF16),
        "lam": jnp.stack([lam_q1, lam_k1, lam_q2, lam_k2], axis=1),
        "g_subln": g_subln.reshape(depth, 1, V_HEAD_DIM), "g_subln_t": g_subln.reshape(depth, V_HEAD_DIM, 1),
        "w_dw": w_dw_p, "b_dw": b_dw.reshape(depth, 1, CONV_WIDTH),
        "g_ln": g_ln.reshape(depth, 1, CONV_WIDTH), "b_ln": b_ln.reshape(depth, 1, CONV_WIDTH)}

    ang = _rope_angles(jnp.arange(seq, dtype=jnp.int32)).T
    params_p = [shared[n] for n in _PROMPT_PRM]
    tab_spec = pl.BlockSpec((ROT_HALF, tile), lambda b, i: (0, i))
    yp, kp_t, vp, cp = pl.pallas_call(
        functools.partial(_prompt_kernel, depth, tile),
        grid=(batch, n_tiles),
        in_specs=[pl.BlockSpec((1, tile, D_MODEL), lambda b, i: (b, i, 0)),
                  pl.BlockSpec((depth, 1, 3, D_MODEL), lambda b, i: (0, b, 0, 0)),
                  tab_spec, tab_spec]
                 + [_const_spec(p.shape) for p in params_p],
        out_specs=[pl.BlockSpec((1, tile, D_MODEL), lambda b, i: (b, i, 0)),
                   pl.BlockSpec((depth, 1, QK_WIDTH, tile), lambda b, i: (0, b, 0, i)),
                   pl.BlockSpec((depth, 1, tile * N_HEADS, V_HEAD_DIM), lambda b, i: (0, b, i, 0)),
                   pl.BlockSpec((depth, 1, CONV_STATE, CONV_WIDTH), lambda b, i: (0, b, 0, 0))],
        out_shape=[jax.ShapeDtypeStruct((batch, seq, D_MODEL), F32),
                   jax.ShapeDtypeStruct((depth, batch, QK_WIDTH, seq), F32),
                   jax.ShapeDtypeStruct((depth, batch, seq * N_HEADS, V_HEAD_DIM), F32),
                   jax.ShapeDtypeStruct((depth, batch, CONV_STATE, CONV_WIDTH), F32)],
        scratch_shapes=[pltpu.VMEM((depth, n_tiles, tile, QK_WIDTH), BF16),
                        pltpu.VMEM((depth, n_tiles, N_HEADS * VT_HEAD_ROWS, tile), BF16),
                        pltpu.VMEM((depth, CONV_PAD + tile, CONV_WIDTH), F32),
                        pltpu.VMEM((SUBLANES - 1, CONV_PAD + tile, CONV_WIDTH), F32),
                        pltpu.VMEM((2, QK_WIDTH, tile), BF16),
                        pltpu.VMEM((2 * N_HEADS, VT_HEAD_ROWS, tile), F32),
                        pltpu.VMEM((2 * N_HEADS, 1, tile), F32),
                        pltpu.VMEM((2 * N_HEADS, 1, tile), F32),
                        pltpu.VMEM((2, 2 * N_HEADS, tile, tile), F32),
                        pltpu.VMEM((2 * N_HEADS, tile, tile), BF16)],
        compiler_params=pltpu.CompilerParams(dimension_semantics=("arbitrary", "arbitrary"),
                                             vmem_limit_bytes=VMEM_LIMIT_BYTES),
        name="prompt_layers",
    )(x_prompt, mod_p, jnp.cos(ang), jnp.sin(ang), *params_p)
    kp = jnp.transpose(kp_t.reshape(depth, batch, N_HEADS, 2, QK_HEAD_DIM, seq), (0, 1, 5, 2, 3, 4))

    tabs_s = _rope_tables_rowmajor(past_len + jnp.arange(dec_seq, dtype=jnp.int32))
    params_s = [shared[n] for n in _SAMPLE_PRM]

    def layer_spec(shape):
        nd = len(shape)
        return pl.BlockSpec((1,) + tuple(shape[1:]), lambda l, b: (l,) + (0,) * (nd - 1))

    lb_spec = lambda shape: pl.BlockSpec((1, 1) + tuple(shape[2:]), lambda l, b: (l, b) + (0,) * (len(shape) - 2))
    ck_t = jnp.transpose(cache_k, (0, 1, 3, 4, 5, 2)).reshape(depth, dec_batch, QK_WIDTH, past_len)
    cv = cache_v.reshape(depth, dec_batch, past_len * N_HEADS, V_HEAD_DIM)
    row_spec = pl.BlockSpec((1, dec_seq, D_MODEL), lambda l, b: (b, 0, 0))
    stab_spec = pl.BlockSpec((dec_seq, LANES), lambda l, b: (0, 0))
    ys, ks, vs, cs = pl.pallas_call(
        functools.partial(_sample_kernel, depth, dec_seq, past_len),
        grid=(depth, dec_batch),
        in_specs=[row_spec, lb_spec(mod_s.shape), stab_spec, stab_spec, stab_spec,
                  lb_spec(ck_t.shape), lb_spec(cv.shape), lb_spec(state_conv.shape)]
                 + [layer_spec(p.shape) for p in params_s],
        out_specs=[pl.BlockSpec((1, dec_seq, D_MODEL), lambda l, b: (jnp.where(l == depth - 1, b, 0), 0, 0)),
                   lb_spec((depth, dec_batch, dec_seq, QK_WIDTH)),
                   lb_spec((depth, dec_batch, dec_seq, ATTN_WIDTH)), lb_spec(state_conv.shape)],
        out_shape=[jax.ShapeDtypeStruct((dec_batch, dec_seq, D_MODEL), F32),
                   jax.ShapeDtypeStruct((depth, dec_batch, dec_seq, QK_WIDTH), F32),
                   jax.ShapeDtypeStruct((depth, dec_batch, dec_seq, ATTN_WIDTH), F32),
                   jax.ShapeDtypeStruct(state_conv.shape, F32)],
        scratch_shapes=[pltpu.VMEM((past_len // KEY_TILE, QK_WIDTH, KEY_TILE), BF16),
                        pltpu.VMEM((past_len, ATTN_WIDTH), BF16),
                        pltpu.VMEM((1, CONV_PAD + dec_seq, CONV_WIDTH), F32),
                        pltpu.VMEM((SUBLANES - 1, CONV_PAD + dec_seq, CONV_WIDTH), F32),
                        pltpu.VMEM((2, dec_seq, QK_WIDTH), BF16),
                        pltpu.VMEM((2 * N_HEADS, dec_seq, V_HEAD_DIM), F32),
                        pltpu.VMEM((2 * N_HEADS, dec_seq, 1), F32), pltpu.VMEM((2 * N_HEADS, dec_seq, 1), F32),
                        pltpu.VMEM((dec_seq, D_MODEL), F32), pltpu.VMEM((dec_batch, dec_seq, D_MODEL), F32)],
        compiler_params=pltpu.CompilerParams(dimension_semantics=("arbitrary", "arbitrary"),
                                             vmem_limit_bytes=VMEM_LIMIT_BYTES),
        name="sample_layers",
    )(x_sample, mod_s, *tabs_s, ck_t, cv, state_conv, *params_s)

    return (yp, ys, kp, vp.reshape(depth, batch, seq, N_HEADS, V_HEAD_DIM), cp,
            ks.reshape(depth, dec_batch, dec_seq, N_HEADS, 2, QK_HEAD_DIM),
            vs.reshape(depth, dec_batch, dec_seq, N_HEADS, V_HEAD_DIM), cs)
```

```python
import functools
import math

import jax
import jax.numpy as jnp
from jax import lax
from jax.experimental import pallas as pl
from jax.experimental.pallas import tpu as pltpu

F32 = jnp.float32
BF16 = jnp.bfloat16

D_MODEL = 1024
N_HEADS = 4
QK_HEAD_DIM = 64
V_HEAD_DIM = 128
ATTN_WIDTH = N_HEADS * V_HEAD_DIM
QK_WIDTH = N_HEADS * 2 * QK_HEAD_DIM
CONV_WIDTH = D_MODEL - ATTN_WIDTH
CONV_KERNEL = 31
CONV_STATE = CONV_KERNEL - 1
CHUNK = 64
ROT_DIM = 16
ROT_HALF = ROT_DIM // 2
ROPE_THETA = 500000.0
NORM_EPS = 1e-6
SUBLN_EPS = 1e-5
LN_EPS = 1e-5
COL_Q, COL_K, COL_V, COL_GA = 0, QK_WIDTH, 2 * QK_WIDTH, 2 * QK_WIDTH + ATTN_WIDTH
COL_GLU_A = 2 * QK_WIDTH + 2 * ATTN_WIDTH
COL_GLU_B = COL_GLU_A + CONV_WIDTH
COL_GC = COL_GLU_B + CONV_WIDTH
IN_COLS = COL_GC + CONV_WIDTH
WT_Q, WT_K, WT_GA = 0, QK_WIDTH, 2 * QK_WIDTH
WR_V, WR_GLU_A, WR_GLU_B, WR_GC = 0, ATTN_WIDTH, ATTN_WIDTH + CONV_WIDTH, ATTN_WIDTH + 2 * CONV_WIDTH

LANES = 128
SUBLANES = 8
CONV_PAD = 32
CONV_ROW_BLOCK = 128
MASK_VALUE = -1e30
LOG2E = math.log2(math.e)
PROMPT_TILE = 256
KEY_TILE = 256
VMEM_LIMIT_BYTES = 60000 * 1024
VT_HEAD_ROWS = V_HEAD_DIM + 16


def _silu(x):
    return x * jax.nn.sigmoid(x)


def _dot(a, b):
    return jnp.dot(a, b, preferred_element_type=F32)


def _dot_nt(a, b):
    return lax.dot_general(a, b, (((1,), (1,)), ((), ())), preferred_element_type=F32)


def _lam_init(layer_idx):
    return 0.8 - 0.6 * math.exp(-0.3 * layer_idx)


def _lam(lam_p, lam_init):
    return (jnp.exp(jnp.sum(lam_p[0:1] * lam_p[1:2], axis=1, keepdims=True))
            - jnp.exp(jnp.sum(lam_p[2:3] * lam_p[3:4], axis=1, keepdims=True)) + lam_init)


def _modulated_norm(x, g_pre, shift, scale):
    ms = jnp.mean(x * x, axis=-1, keepdims=True)
    return ((x * lax.rsqrt(ms + NORM_EPS)) * (g_pre * (1.0 + scale)) + shift).astype(BF16)


def _conv_taps(u_s, sl, sh_s, rows, wdw_ref, bdw):
    base = CONV_PAD - CONV_STATE
    row_block = min(rows, CONV_ROW_BLOCK)
    n_sh = row_block + CONV_PAD - SUBLANES
    out_rows = []
    for rb in range(rows // row_block):
        for r in range(1, SUBLANES):
            sh_s[r - 1, 0:n_sh, :] = u_s[sl, pl.ds(rb * row_block + r, n_sh), :]
        cols = []
        for c in range(CONV_WIDTH // LANES):
            lanes = slice(c * LANES, (c + 1) * LANES)
            acc = jnp.broadcast_to(bdw[:, lanes], (row_block, LANES))
            for o in range(base, base + CONV_KERNEL):
                r, q8 = o % SUBLANES, o - o % SUBLANES
                if r == 0:
                    win = u_s[sl, rb * row_block + q8:rb * row_block + q8 + row_block, lanes]
                else:
                    win = sh_s[r - 1, q8:q8 + row_block, lanes]
                acc = acc + wdw_ref[sl, o - base:o - base + 1, lanes] * win
            cols.append(acc)
        out_rows.append(jnp.concatenate(cols, axis=1))
    return jnp.concatenate(out_rows, axis=0)


def _conv_branch(hb, w_r, sl, prm, u_s, sh_s):
    rows = hb.shape[0]
    u = (_dot(hb, w_r[sl, :, WR_GLU_A:WR_GLU_A + CONV_WIDTH])
         * jax.nn.sigmoid(_dot(hb, w_r[sl, :, WR_GLU_B:WR_GLU_B + CONV_WIDTH])))
    u_s[sl, CONV_PAD:CONV_PAD + rows, :] = u
    conv = _conv_taps(u_s, sl, sh_s, rows, prm["w_dw"], prm["b_dw"][sl])
    mu = jnp.mean(conv, axis=-1, keepdims=True)
    xc = conv - mu
    ln = xc * lax.rsqrt(jnp.mean(xc * xc, axis=-1, keepdims=True) + LN_EPS) * prm["g_ln"][sl] + prm["b_ln"][sl]
    gc = _dot(hb, w_r[sl, :, WR_GC:WR_GC + CONV_WIDTH])
    return (_silu(ln) * _silu(gc)).astype(BF16)


def _out_proj(x, attn_out, conv_out, w_out, sl, g_post, gate):
    m = _dot(attn_out, w_out[sl, 0:ATTN_WIDTH, :]) + _dot(conv_out, w_out[sl, ATTN_WIDTH:ATTN_WIDTH + CONV_WIDTH, :])
    mn = m * lax.rsqrt(jnp.mean(m * m, axis=-1, keepdims=True) + NORM_EPS) * g_post
    return x + gate * mn


def _rope_t(zt, cos8, sin8):
    parts = []
    for mb in range(zt.shape[0] // QK_HEAD_DIM):
        b0 = mb * QK_HEAD_DIM
        x1, x2 = zt[b0:b0 + ROT_HALF], zt[b0 + ROT_HALF:b0 + ROT_DIM]
        parts += [x1 * cos8 - x2 * sin8, x2 * cos8 + x1 * sin8, zt[b0 + ROT_DIM:b0 + QK_HEAD_DIM]]
    return jnp.concatenate(parts, axis=0)


def _attention_t(q_s, tile_idx, k_s, vt_s, kl, att):
    acc_s, m_s, a_s, s_s, p_s = att
    tile = p_s.shape[1]
    chunk_shift = int(math.log2(CHUNK))

    def scores(j, buf):
        for h in range(N_HEADS):
            hs = slice(h * V_HEAD_DIM, (h + 1) * V_HEAD_DIM)
            kh = k_s[kl, j, :, hs]
            for mp in range(2):
                s_s[buf, 2 * h + mp] = _dot(kh, q_s[mp, hs, :])

    def softmax(first, buf):
        for idx in range(2 * N_HEADS):
            st = s_s[buf, idx]
            if first:
                k_chunk = lax.shift_right_logical(lax.broadcasted_iota(jnp.int32, (tile, tile), 0), chunk_shift)
                q_chunk = lax.shift_right_logical(lax.broadcasted_iota(jnp.int32, (tile, tile), 1), chunk_shift)
                st = jnp.where(k_chunk <= q_chunk, st, MASK_VALUE)
                m_new = jnp.max(st, axis=0, keepdims=True)
                a_s[idx] = jnp.ones((1, tile), F32)
            else:
                m_old = m_s[idx]
                m_new = jnp.maximum(m_old, jnp.max(st, axis=0, keepdims=True))
                a_s[idx] = jnp.exp2(m_old - m_new)
            m_s[idx] = m_new
            p_s[idx] = jnp.exp2(st - m_new).astype(BF16)

    def numerators(j):
        for h in range(N_HEADS):
            vth = vt_s[kl, j, h * VT_HEAD_ROWS:(h + 1) * VT_HEAD_ROWS, :]
            for mp in range(2):
                idx = 2 * h + mp
                acc_s[idx] = a_s[idx] * acc_s[idx] + _dot(vth, p_s[idx])

    last_past = jnp.maximum(tile_idx - 1, 0)
    acc_s[...] = jnp.zeros(acc_s.shape, F32)
    scores(tile_idx, 0)
    scores(0, 1)
    softmax(True, 0)

    def stage(j, buf):
        numerators(jnp.where(j == 0, tile_idx, j - 1))
        scores(jnp.minimum(j + 1, last_past), buf)
        softmax(False, 1 - buf)

    def body(jj, carry):
        stage(2 * jj, 0)
        stage(2 * jj + 1, 1)
        return carry

    lax.fori_loop(0, lax.shift_right_logical(tile_idx, 1), body, 0)

    @pl.when((tile_idx & 1) == 1)
    def _():
        stage(tile_idx - 1, 0)

    numerators(jnp.where(tile_idx == 0, tile_idx, last_past))


def _prompt_pre(x, l, i, mod_ref, prm, cos8, sin8, put_k, put_v, scr):
    k_s, vt_s, u_s, sh_s, q_s = scr[:5]
    rows = x.shape[0]
    shift, scale, gate = mod_ref[0, 0, 0:1, :], mod_ref[0, 0, 1:2, :], mod_ref[0, 0, 2:3, :]
    hb = _modulated_norm(x, prm["g_pre"][l], shift, scale)
    w_t, w_r = prm["w_t"], prm["w_r"]
    conv_out = _conv_branch(hb, w_r, l, prm, u_s, sh_s)
    conv_state = u_s[l, rows + CONV_PAD - CONV_STATE:rows + CONV_PAD, :]
    u_s[l, 0:CONV_PAD, :] = u_s[l, rows:rows + CONV_PAD, :]

    qt = _rope_t(_dot_nt(w_t[l, WT_Q:WT_Q + QK_WIDTH, :], hb), cos8, sin8) * (QK_HEAD_DIM ** -0.5 * LOG2E)
    zero = jnp.zeros((QK_HEAD_DIM, rows), F32)
    q0, q1 = [], []
    for mb in range(QK_WIDTH // QK_HEAD_DIM):
        blk = qt[mb * QK_HEAD_DIM:(mb + 1) * QK_HEAD_DIM]
        q0.append(blk if mb % 2 == 0 else zero)
        q1.append(zero if mb % 2 == 0 else blk)
    q_s[l, 0] = jnp.concatenate(q0, axis=0).astype(BF16)
    q_s[l, 1] = jnp.concatenate(q1, axis=0).astype(BF16)

    kt = _rope_t(_dot_nt(w_t[l, WT_K:WT_K + QK_WIDTH, :], hb), cos8, sin8)
    put_k(kt)
    k_s[l, i] = kt.T.astype(BF16)
    v = _dot(hb, w_r[l, :, WR_V:WR_V + ATTN_WIDTH])
    put_v(v)
    vt = v.T
    ones = jnp.ones((VT_HEAD_ROWS - V_HEAD_DIM, rows), F32)
    vt_s[l, i] = jnp.concatenate(
        [blk for h in range(N_HEADS) for blk in (vt[h * V_HEAD_DIM:(h + 1) * V_HEAD_DIM], ones)], axis=0).astype(BF16)
    return x, gate, hb, conv_out, conv_state


def _prompt_post(state, l, prm, acc_s):
    x, gate, hb, conv_out, _ = state
    lam_init = _lam_init(l)
    lam = _lam(prm["lam"][l], lam_init)
    gat = _dot_nt(prm["w_t"][l, WT_GA:WT_GA + ATTN_WIDTH, :], hb)
    g_sub = prm["g_subln_t"][l]
    heads = []
    for hd in range(N_HEADS):
        hs = slice(hd * V_HEAD_DIM, (hd + 1) * V_HEAD_DIM)
        n0, n1 = acc_s[l, 2 * hd], acc_s[l, 2 * hd + 1]
        l0, l1 = n0[V_HEAD_DIM:V_HEAD_DIM + 1], n1[V_HEAD_DIM:V_HEAD_DIM + 1]
        o = n0[0:V_HEAD_DIM] * (1.0 / l0) - n1[0:V_HEAD_DIM] * (lam / l1)
        o = o * lax.rsqrt(jnp.mean(o * o, axis=0, keepdims=True) + SUBLN_EPS) * g_sub
        heads.append((o * (1.0 - lam_init)) * _silu(gat[hs]))
    attn_out = jnp.concatenate(heads, axis=0).T.astype(BF16)
    return _out_proj(x, attn_out, conv_out, prm["w_out"], l, prm["g_post"][l], gate)


_PROMPT_PRM = ("g_pre", "g_post", "w_t", "w_r", "w_out", "lam", "g_subln_t", "w_dw", "b_dw", "g_ln", "b_ln")


def _prompt_kernel(n_tiles, n_flat, x_ref, mod0_ref, mod1_ref, cos0_ref, sin0_ref, cos1_ref, sin1_ref, *rest):
    n_prm = len(_PROMPT_PRM)
    prm = dict(zip(_PROMPT_PRM, rest[:n_prm]))
    y_ref, k_ref, v_ref, conv_ref = rest[n_prm:n_prm + 4]
    scr = rest[n_prm + 4:]
    k_s, vt_s, u_s, sh_s, q_s, acc_s, m_s, a_s, s_s, p_s, y0_s, kt_hold, v_hold, cs_hold = scr
    s = pl.program_id(0)
    i0 = lax.rem(jnp.minimum(s, n_flat - 1), n_tiles)
    i1 = lax.rem(jnp.maximum(s - 1, 0), n_tiles)
    tile = x_ref.shape[1]

    @pl.when(s == 0)
    def _():
        y0_s[...] = jnp.zeros(y0_s.shape, F32)
        kt_hold[...] = jnp.zeros(kt_hold.shape, F32)
        v_hold[...] = jnp.zeros(v_hold.shape, F32)
        cs_hold[...] = jnp.zeros(cs_hold.shape, F32)

    @pl.when(i0 == 0)
    def _():
        u_s[0, 0:CONV_PAD, :] = jnp.zeros((CONV_PAD, CONV_WIDTH), F32)

    @pl.when(i1 == 0)
    def _():
        u_s[1, 0:CONV_PAD, :] = jnp.zeros((CONV_PAD, CONV_WIDTH), F32)

    def store_v(layer, v):
        for h in range(N_HEADS):
            v_ref[layer, 0, pl.ds(h, tile, stride=N_HEADS), :] = v[:, h * V_HEAD_DIM:(h + 1) * V_HEAD_DIM]

    k_ref[0, 0] = kt_hold[...]
    store_v(0, v_hold[...])
    conv_ref[0, 0] = cs_hold[...]

    def hold_k(kt):
        kt_hold[...] = kt

    def hold_v(v):
        v_hold[...] = v

    def out_k(kt):
        k_ref[1, 0] = kt

    st0 = _prompt_pre(x_ref[0], 0, i0, mod0_ref, prm, cos0_ref[...], sin0_ref[...], hold_k, hold_v, scr)
    cs_hold[...] = st0[4]
    st1 = _prompt_pre(y0_s[...], 1, i1, mod1_ref, prm, cos1_ref[...], sin1_ref[...], out_k,
                      functools.partial(store_v, 1), scr)
    conv_ref[1, 0] = st1[4]

    _attention_t(q_s.at[0], i0, k_s, vt_s, 0, (acc_s.at[0], m_s, a_s, s_s, p_s))
    _attention_t(q_s.at[1], i1, k_s, vt_s, 1, (acc_s.at[1], m_s, a_s, s_s, p_s))

    y0_s[...] = _prompt_post(st0, 0, prm, acc_s)
    y_ref[0] = _prompt_post(st1, 1, prm, acc_s)


def _rope(z, cos_t, sa_t, sb_t):
    blocks = []
    for c in range(z.shape[1] // LANES):
        zc = z[:, c * LANES:(c + 1) * LANES]
        up = pltpu.roll(zc, LANES - ROT_HALF, 1)
        down = pltpu.roll(zc, ROT_HALF, 1)
        blocks.append(zc * cos_t + up * sa_t + down * sb_t)
    return jnp.concatenate(blocks, axis=1)


def _attention(q_s, k_own, v_own, n_past, kt_s, v_s, acc_s, m_s, l_s):
    for h in range(N_HEADS):
        hs = slice(h * V_HEAD_DIM, (h + 1) * V_HEAD_DIM)
        for mp in range(2):
            idx = 2 * h + mp
            s = _dot_nt(q_s[mp, :, hs], k_own[:, hs])
            mx = jnp.max(s, axis=1, keepdims=True)
            p = jnp.exp(s - mx)
            m_s[idx] = mx
            l_s[idx] = jnp.sum(p, axis=1, keepdims=True)
            acc_s[idx] = _dot(p.astype(BF16), v_own[:, hs])

    def body(j, carry):
        r0 = pl.multiple_of(j * KEY_TILE, KEY_TILE)
        for h in range(N_HEADS):
            hs = slice(h * V_HEAD_DIM, (h + 1) * V_HEAD_DIM)
            kth = kt_s[j, hs, :]
            vt = v_s[pl.ds(r0, KEY_TILE), hs]
            for mp in range(2):
                idx = 2 * h + mp
                s = _dot(q_s[mp, :, hs], kth)
                m_old = m_s[idx]
                m_new = jnp.maximum(m_old, jnp.max(s, axis=1, keepdims=True))
                alpha = jnp.exp(m_old - m_new)
                p = jnp.exp(s - m_new)
                l_s[idx] = alpha * l_s[idx] + jnp.sum(p, axis=1, keepdims=True)
                acc_s[idx] = alpha * acc_s[idx] + _dot(p.astype(BF16), vt)
                m_s[idx] = m_new
        return carry

    lax.fori_loop(0, n_past, body, 0)


_SAMPLE_PRM = ("g_pre", "g_post", "w_t", "w_r", "w_out", "lam", "g_subln", "w_dw", "b_dw", "g_ln", "b_ln")


def _sample_kernel(depth, rows, past_len, x_ref, mod_ref, cos_ref, sa_ref, sb_ref, ck_ref, cv_ref, cs_ref, *rest):
    n_prm = len(_SAMPLE_PRM)
    prm = dict(zip(_SAMPLE_PRM, rest[:n_prm]))
    y_ref, k_ref, v_ref, conv_ref = rest[n_prm:n_prm + 4]
    kt_s, v_s, u_s, sh_s, q_s, acc_s, m_s, l_s, xin_s, y1_s = rest[n_prm + 4:]
    l = pl.program_id(0)
    b = pl.program_id(1)

    @pl.when(l == 0)
    def _():
        xin_s[...] = x_ref[0]

    @pl.when(l != 0)
    def _():
        xin_s[...] = y1_s[b]

    for j in range(past_len // KEY_TILE):
        kt_s[j] = ck_ref[0, 0, :, j * KEY_TILE:(j + 1) * KEY_TILE].astype(BF16)
    for h in range(N_HEADS):
        v_s[:, h * V_HEAD_DIM:(h + 1) * V_HEAD_DIM] = cv_ref[0, 0, pl.ds(h, past_len, stride=N_HEADS), :].astype(BF16)
    u_s[0, 0:CONV_PAD, :] = jnp.zeros((CONV_PAD, CONV_WIDTH), F32)
    u_s[0, CONV_PAD - CONV_STATE:CONV_PAD, :] = cs_ref[0, 0]

    x = xin_s[...]
    lam_init = jnp.where(l == 0, _lam_init(0), _lam_init(1)).astype(F32)
    shift, scale, gate = mod_ref[0, 0, 0:1, :], mod_ref[0, 0, 1:2, :], mod_ref[0, 0, 2:3, :]
    hb = _modulated_norm(x, prm["g_pre"][0], shift, scale)
    w_t, w_r = prm["w_t"], prm["w_r"]
    tabs = (cos_ref[...], sa_ref[...], sb_ref[...])

    q = _rope(_dot_nt(hb, w_t[0, WT_Q:WT_Q + QK_WIDTH, :]), *tabs) * (QK_HEAD_DIM ** -0.5)
    first_map = (lax.broadcasted_iota(jnp.int32, (rows, QK_WIDTH), 1) & (V_HEAD_DIM - 1)) < QK_HEAD_DIM
    q_s[0] = jnp.where(first_map, q, 0.0).astype(BF16)
    q_s[1] = jnp.where(first_map, 0.0, q).astype(BF16)
    k = _rope(_dot_nt(hb, w_t[0, WT_K:WT_K + QK_WIDTH, :]), *tabs)
    v = _dot(hb, w_r[0, :, WR_V:WR_V + ATTN_WIDTH])
    k_ref[0, 0] = k
    v_ref[0, 0] = v
    _attention(q_s, k.astype(BF16), v.astype(BF16), past_len // KEY_TILE, kt_s, v_s, acc_s, m_s, l_s)

    lam = _lam(prm["lam"][0], lam_init)
    ga = _dot_nt(hb, w_t[0, WT_GA:WT_GA + ATTN_WIDTH, :])
    g_sub = prm["g_subln"][0]
    heads = []
    for hd in range(N_HEADS):
        hs = slice(hd * V_HEAD_DIM, (hd + 1) * V_HEAD_DIM)
        o = acc_s[2 * hd] * (1.0 / l_s[2 * hd]) - acc_s[2 * hd + 1] * (lam / l_s[2 * hd + 1])
        o = o * lax.rsqrt(jnp.mean(o * o, axis=-1, keepdims=True) + SUBLN_EPS) * g_sub
        heads.append(((o * (1.0 - lam_init)) * _silu(ga[:, hs])).astype(BF16))
    attn_out = jnp.concatenate(heads, axis=1)

    conv_out = _conv_branch(hb, w_r, 0, prm, u_s, sh_s)
    y = _out_proj(x, attn_out, conv_out, prm["w_out"], 0, prm["g_post"][0], gate)
    conv_ref[0, 0] = u_s[0, rows + CONV_PAD - CONV_STATE:rows + CONV_PAD, :]
    y1_s[b] = y

    @pl.when(l == depth - 1)
    def _():
        y_ref[0] = y


def _mod_kernel(c_ref, w_ref, b_ref, o_ref):
    c = c_ref[...]
    o_ref[0] = _dot(_silu(c), w_ref[0]) + b_ref[0]


def _rope_angles(pos):
    inv = jnp.float32(ROPE_THETA) ** (-jnp.arange(0, ROT_DIM, 2, dtype=F32) / ROT_DIM)
    return pos.astype(F32)[:, None] * inv[None, :]


def _rope_tables_rowmajor(pos):
    ang = _rope_angles(pos)
    lane = jnp.arange(LANES, dtype=jnp.int32) % QK_HEAD_DIM
    ang = ang[:, lane % ROT_HALF]
    cos, sin = jnp.cos(ang), jnp.sin(ang)
    lo = (lane < ROT_HALF)[None, :]
    hi = ((lane >= ROT_HALF) & (lane < ROT_DIM))[None, :]
    return (jnp.where(lo | hi, cos, 1.0), jnp.where(lo, -sin, 0.0), jnp.where(hi, sin, 0.0))


def _const_spec(shape):
    nd = len(shape)
    return pl.BlockSpec(shape, lambda *_: (0,) * nd, pipeline_mode=pl.Buffered(1))


def kernel(x_prompt, x_sample, c_prompt, c_sample, cache_k, cache_v, state_conv, w_ada, b_ada, g_pre, g_post, w_in,
           w_out, lam_q1, lam_k1, lam_q2, lam_k2, g_subln, w_dw, b_dw, g_ln, b_ln):
    batch, seq, _ = x_prompt.shape
    dec_batch, dec_seq, _ = x_sample.shape
    depth = w_in.shape[0]
    past_len = cache_k.shape[2]
    tile = min(PROMPT_TILE, seq)
    n_tiles = seq // tile
    assert seq % tile == 0 and tile % LANES == 0 and past_len % KEY_TILE == 0 and dec_seq % SUBLANES == 0

    c_all = jnp.concatenate([c_prompt, c_sample], axis=0)
    n_c = c_all.shape[0]
    mod_cols = 512
    mod = pl.pallas_call(
        _mod_kernel,
        grid=(depth, 3 * D_MODEL // mod_cols),
        in_specs=[pl.BlockSpec((n_c, D_MODEL), lambda l, n: (0, 0)),
                  pl.BlockSpec((1, D_MODEL, mod_cols), lambda l, n: (l, 0, n)),
                  pl.BlockSpec((1, 1, mod_cols), lambda l, n: (l, 0, n))],
        out_specs=pl.BlockSpec((1, n_c, mod_cols), lambda l, n: (l, 0, n)),
        out_shape=jax.ShapeDtypeStruct((depth, n_c, 3 * D_MODEL), F32),
        name="adaln_mod",
    )(c_all, w_ada, b_ada.reshape(depth, 1, 3 * D_MODEL))
    mod_p = mod[:, :batch].reshape(depth, batch, 3, D_MODEL)
    mod_s = mod[:, batch:].reshape(depth, dec_batch, 3, D_MODEL)

    w_in_b = w_in.astype(BF16)
    w_t = jnp.swapaxes(jnp.concatenate([w_in_b[:, :, COL_Q:COL_V], w_in_b[:, :, COL_GA:COL_GLU_A]], axis=2), 1, 2)
    w_r = jnp.concatenate([w_in_b[:, :, COL_V:COL_GA], w_in_b[:, :, COL_GLU_A:IN_COLS]], axis=2)
    w_dw_p = jnp.concatenate([w_dw, jnp.zeros((depth, CONV_PAD - CONV_KERNEL, CONV_WIDTH), w_dw.dtype)], axis=1)
    shared = {
        "g_pre": g_pre.reshape(depth, 1, D_MODEL), "g_post": g_post.reshape(depth, 1, D_MODEL),
        "w_t": w_t, "w_r": w_r, "w_out": w_out.astype(BF16),
        "lam": jnp.stack([lam_q1, lam_k1, lam_q2, lam_k2], axis=1),
        "g_subln": g_subln.reshape(depth, 1, V_HEAD_DIM), "g_subln_t": g_subln.reshape(depth, V_HEAD_DIM, 1),
        "w_dw": w_dw_p, "b_dw": b_dw.reshape(depth, 1, CONV_WIDTH),
        "g_ln": g_ln.reshape(depth, 1, CONV_WIDTH), "b_ln": b_ln.reshape(depth, 1, CONV_WIDTH)}

    assert depth == 2, "the prompt call interleaves exactly two layers"
    n_flat = batch * n_tiles
    ang = _rope_angles(jnp.arange(seq, dtype=jnp.int32)).T
    cos_p, sin_p = jnp.cos(ang), jnp.sin(ang)
    params_p = [shared[n] for n in _PROMPT_PRM]

    def tile0(s):
        f = jnp.minimum(s, n_flat - 1)
        return f // n_tiles, f % n_tiles

    def tile1(s):
        f = jnp.maximum(s - 1, 0)
        return f // n_tiles, f % n_tiles

    mod_block = (1, 1, 3, D_MODEL)
    tab_block = (ROT_HALF, tile)
    sh_rows = min(tile, CONV_ROW_BLOCK) + CONV_PAD - SUBLANES
    yp, kp_t, vp, cp = pl.pallas_call(
        functools.partial(_prompt_kernel, n_tiles, n_flat),
        grid=(n_flat + 1,),
        in_specs=[pl.BlockSpec((1, tile, D_MODEL), lambda s: tile0(s) + (0,)),
                  pl.BlockSpec(mod_block, lambda s: (0, tile0(s)[0], 0, 0)),
                  pl.BlockSpec(mod_block, lambda s: (1, tile1(s)[0], 0, 0)),
                  pl.BlockSpec(tab_block, lambda s: (0, tile0(s)[1])),
                  pl.BlockSpec(tab_block, lambda s: (0, tile0(s)[1])),
                  pl.BlockSpec(tab_block, lambda s: (0, tile1(s)[1])),
                  pl.BlockSpec(tab_block, lambda s: (0, tile1(s)[1]))]
                 + [_const_spec(p.shape) for p in params_p],
        out_specs=[pl.BlockSpec((1, tile, D_MODEL), lambda s: tile1(s) + (0,)),
                   pl.BlockSpec((depth, 1, QK_WIDTH, tile), lambda s: (0, tile1(s)[0], 0, tile1(s)[1])),
                   pl.BlockSpec((depth, 1, tile * N_HEADS, V_HEAD_DIM), lambda s: (0,) + tile1(s) + (0,)),
                   pl.BlockSpec((depth, 1, CONV_STATE, CONV_WIDTH), lambda s: (0, tile1(s)[0], 0, 0))],
        out_shape=[jax.ShapeDtypeStruct((batch, seq, D_MODEL), F32),
                   jax.ShapeDtypeStruct((depth, batch, QK_WIDTH, seq), F32),
                   jax.ShapeDtypeStruct((depth, batch, seq * N_HEADS, V_HEAD_DIM), F32),
                   jax.ShapeDtypeStruct((depth, batch, CONV_STATE, CONV_WIDTH), F32)],
        scratch_shapes=[pltpu.VMEM((depth, n_tiles, tile, QK_WIDTH), BF16),
                        pltpu.VMEM((depth, n_tiles, N_HEADS * VT_HEAD_ROWS, tile), BF16),
                        pltpu.VMEM((depth, CONV_PAD + tile, CONV_WIDTH), F32),
                        pltpu.VMEM((SUBLANES - 1, sh_rows, CONV_WIDTH), F32),
                        pltpu.VMEM((depth, 2, QK_WIDTH, tile), BF16),
                        pltpu.VMEM((depth, 2 * N_HEADS, VT_HEAD_ROWS, tile), F32),
                        pltpu.VMEM((2 * N_HEADS, 1, tile), F32),
                        pltpu.VMEM((2 * N_HEADS, 1, tile), F32),
                        pltpu.VMEM((2, 2 * N_HEADS, tile, tile), F32),
                        pltpu.VMEM((2 * N_HEADS, tile, tile), BF16),
                        pltpu.VMEM((tile, D_MODEL), F32),
                        pltpu.VMEM((QK_WIDTH, tile), F32),
                        pltpu.VMEM((tile, ATTN_WIDTH), F32),
                        pltpu.VMEM((CONV_STATE, CONV_WIDTH), F32)],
        compiler_params=pltpu.CompilerParams(dimension_semantics=("arbitrary",),
                                             vmem_limit_bytes=VMEM_LIMIT_BYTES),
        name="prompt_layers",
    )(x_prompt, mod_p, mod_p, cos_p, sin_p, cos_p, sin_p, *params_p)
    kp = jnp.transpose(kp_t.reshape(depth, batch, N_HEADS, 2, QK_HEAD_DIM, seq), (0, 1, 5, 2, 3, 4))

    tabs_s = _rope_tables_rowmajor(past_len + jnp.arange(dec_seq, dtype=jnp.int32))
    params_s = [shared[n] for n in _SAMPLE_PRM]

    def layer_spec(shape):
        nd = len(shape)
        return pl.BlockSpec((1,) + tuple(shape[1:]), lambda l, b: (l,) + (0,) * (nd - 1))

    lb_spec = lambda shape: pl.BlockSpec((1, 1) + tuple(shape[2:]), lambda l, b: (l, b) + (0,) * (len(shape) - 2))
    ck_t = jnp.transpose(cache_k, (0, 1, 3, 4, 5, 2)).reshape(depth, dec_batch, QK_WIDTH, past_len)
    cv = cache_v.reshape(depth, dec_batch, past_len * N_HEADS, V_HEAD_DIM)
    row_spec = pl.BlockSpec((1, dec_seq, D_MODEL), lambda l, b: (b, 0, 0))
    stab_spec = pl.BlockSpec((dec_seq, LANES), lambda l, b: (0, 0))
    ys, ks, vs, cs = pl.pallas_call(
        functools.partial(_sample_kernel, depth, dec_seq, past_len),
        grid=(depth, dec_batch),
        in_specs=[row_spec, lb_spec(mod_s.shape), stab_spec, stab_spec, stab_spec,
                  lb_spec(ck_t.shape), lb_spec(cv.shape), lb_spec(state_conv.shape)]
                 + [layer_spec(p.shape) for p in params_s],
        out_specs=[pl.BlockSpec((1, dec_seq, D_MODEL), lambda l, b: (jnp.where(l == depth - 1, b, 0), 0, 0)),
                   lb_spec((depth, dec_batch, dec_seq, QK_WIDTH)),
                   lb_spec((depth, dec_batch, dec_seq, ATTN_WIDTH)), lb_spec(state_conv.shape)],
        out_shape=[jax.ShapeDtypeStruct((dec_batch, dec_seq, D_MODEL), F32),
                   jax.ShapeDtypeStruct((depth, dec_batch, dec_seq, QK_WIDTH), F32),
                   jax.ShapeDtypeStruct((depth, dec_batch, dec_seq, ATTN_WIDTH), F32),
                   jax.ShapeDtypeStruct(state_conv.shape, F32)],
        scratch_shapes=[pltpu.VMEM((past_len // KEY_TILE, QK_WIDTH, KEY_TILE), BF16),
                        pltpu.VMEM((past_len, ATTN_WIDTH), BF16),
                        pltpu.VMEM((1, CONV_PAD + dec_seq, CONV_WIDTH), F32),
                        pltpu.VMEM((SUBLANES - 1, min(dec_seq, CONV_ROW_BLOCK) + CONV_PAD - SUBLANES, CONV_WIDTH),
                                   F32),
                        pltpu.VMEM((2, dec_seq, QK_WIDTH), BF16),
                        pltpu.VMEM((2 * N_HEADS, dec_seq, V_HEAD_DIM), F32),
                        pltpu.VMEM((2 * N_HEADS, dec_seq, 1), F32), pltpu.VMEM((2 * N_HEADS, dec_seq, 1), F32),
                        pltpu.VMEM((dec_seq, D_MODEL), F32), pltpu.VMEM((dec_batch, dec_seq, D_MODEL), F32)],
        compiler_params=pltpu.CompilerParams(dimension_semantics=("arbitrary", "arbitrary"),
                                             vmem_limit_bytes=VMEM_LIMIT_BYTES),
        name="sample_layers",
    )(x_sample, mod_s, *tabs_s, ck_t, cv, state_conv, *params_s)

    return (yp, ys, kp, vp.reshape(depth, batch, seq, N_HEADS, V_HEAD_DIM), cp,
            ks.reshape(depth, dec_batch, dec_seq, N_HEADS, 2, QK_HEAD_DIM),
            vs.reshape(depth, dec_batch, dec_seq, N_HEADS, V_HEAD_DIM), cs)
```

```python
import functools
import math

import jax
import jax.numpy as jnp
from jax import lax
from jax.experimental import pallas as pl
from jax.experimental.pallas import tpu as pltpu

F32 = jnp.float32
BF16 = jnp.bfloat16

D_MODEL = 1024
N_HEADS = 4
QK_HEAD_DIM = 64
V_HEAD_DIM = 128
ATTN_WIDTH = N_HEADS * V_HEAD_DIM
QK_WIDTH = N_HEADS * 2 * QK_HEAD_DIM
CONV_WIDTH = D_MODEL - ATTN_WIDTH
CONV_KERNEL = 31
CONV_STATE = CONV_KERNEL - 1
CHUNK = 64
ROT_DIM = 16
ROT_HALF = ROT_DIM // 2
ROPE_THETA = 500000.0
NORM_EPS = 1e-6
SUBLN_EPS = 1e-5
LN_EPS = 1e-5
COL_Q, COL_K, COL_V, COL_GA = 0, QK_WIDTH, 2 * QK_WIDTH, 2 * QK_WIDTH + ATTN_WIDTH
COL_GLU_A = 2 * QK_WIDTH + 2 * ATTN_WIDTH
COL_GLU_B = COL_GLU_A + CONV_WIDTH
COL_GC = COL_GLU_B + CONV_WIDTH
IN_COLS = COL_GC + CONV_WIDTH
WT_Q, WT_K, WT_GA = 0, QK_WIDTH, 2 * QK_WIDTH
WR_V, WR_GLU_A, WR_GLU_B, WR_GC = 0, ATTN_WIDTH, ATTN_WIDTH + CONV_WIDTH, ATTN_WIDTH + 2 * CONV_WIDTH

LANES = 128
SUBLANES = 8
CONV_PAD = 32
MASK_VALUE = -1e30
LOG2E = math.log2(math.e)
PROMPT_TILE = 256
KEY_TILE = 512
VMEM_LIMIT_BYTES = 60000 * 1024
VT_HEAD_ROWS = V_HEAD_DIM + 16


def _silu(x):
    return x * jax.nn.sigmoid(x)


def _dot(a, b):
    return jnp.dot(a, b, preferred_element_type=F32)


def _dot_nt(a, b):
    return lax.dot_general(a, b, (((1,), (1,)), ((), ())), preferred_element_type=F32)


def _lam_init(layer_idx):
    return 0.8 - 0.6 * math.exp(-0.3 * layer_idx)


def _lam(lam_p, lam_init):
    return (jnp.exp(jnp.sum(lam_p[0:1] * lam_p[1:2], axis=1, keepdims=True))
            - jnp.exp(jnp.sum(lam_p[2:3] * lam_p[3:4], axis=1, keepdims=True)) + lam_init)


def _modulated_norm(x, g_pre, shift, scale):
    ms = jnp.mean(x * x, axis=-1, keepdims=True)
    return ((x * lax.rsqrt(ms + NORM_EPS)) * (g_pre * (1.0 + scale)) + shift).astype(BF16)


def _conv_taps(u_s, sl, sh_s, rows, wdw_ref, bdw):
    total = rows + CONV_PAD
    pitch = total // SUBLANES
    assert total % SUBLANES == 0 and pitch % SUBLANES != 0
    n_blocks = CONV_WIDTH // LANES
    for c in range(n_blocks):
        sh_s[0, c] = u_s[sl, :, c * LANES:(c + 1) * LANES]
    for c in range(n_blocks):
        lanes = slice(c * LANES, (c + 1) * LANES)

        def rows_of(j, c=c):
            return sh_s[0, c, pl.ds(j, SUBLANES, stride=pitch), :]

        moved = {}
        for j in range(-CONV_STATE, 0):
            wraps = -(j // pitch)
            moved[j] = pltpu.roll(rows_of(j + wraps * pitch), wraps, 0)
        accs = [jnp.broadcast_to(bdw[:, lanes], (SUBLANES, LANES))] * pitch
        for d in range(-CONV_STATE, 1):
            w_d = jnp.broadcast_to(wdw_ref[sl, d + CONV_STATE:d + CONV_STATE + 1, lanes], (SUBLANES, LANES))
            accs = [acc + w_d * (rows_of(k + d) if k + d >= 0 else moved[k + d]) for k, acc in enumerate(accs)]
        for k, acc in enumerate(accs):
            sh_s[1, c, pl.ds(k, SUBLANES, stride=pitch), :] = acc
    return jnp.concatenate([sh_s[1, c, CONV_PAD:CONV_PAD + rows, :] for c in range(n_blocks)], axis=1)


def _conv_branch(hb, w_r, sl, prm, u_s, sh_s):
    rows = hb.shape[0]
    u = (_dot(hb, w_r[sl, :, WR_GLU_A:WR_GLU_A + CONV_WIDTH])
         * jax.nn.sigmoid(_dot(hb, w_r[sl, :, WR_GLU_B:WR_GLU_B + CONV_WIDTH])))
    u_s[sl, CONV_PAD:CONV_PAD + rows, :] = u
    conv = _conv_taps(u_s, sl, sh_s, rows, prm["w_dw"], prm["b_dw"][sl])
    mu = jnp.mean(conv, axis=-1, keepdims=True)
    xc = conv - mu
    ln = xc * lax.rsqrt(jnp.mean(xc * xc, axis=-1, keepdims=True) + LN_EPS) * prm["g_ln"][sl] + prm["b_ln"][sl]
    gc = _dot(hb, w_r[sl, :, WR_GC:WR_GC + CONV_WIDTH])
    return (_silu(ln) * _silu(gc)).astype(BF16)


def _out_proj(x, attn_out, conv_out, w_out, sl, g_post, gate):
    m = _dot(attn_out, w_out[sl, 0:ATTN_WIDTH, :]) + _dot(conv_out, w_out[sl, ATTN_WIDTH:ATTN_WIDTH + CONV_WIDTH, :])
    mn = m * lax.rsqrt(jnp.mean(m * m, axis=-1, keepdims=True) + NORM_EPS) * g_post
    return x + gate * mn


def _rope_t(zt, cos8, sin8):
    parts = []
    for mb in range(zt.shape[0] // QK_HEAD_DIM):
        b0 = mb * QK_HEAD_DIM
        x1, x2 = zt[b0:b0 + ROT_HALF], zt[b0 + ROT_HALF:b0 + ROT_DIM]
        parts += [x1 * cos8 - x2 * sin8, x2 * cos8 + x1 * sin8, zt[b0 + ROT_DIM:b0 + QK_HEAD_DIM]]
    return jnp.concatenate(parts, axis=0)


def _attention_t(q_s, tile_idx, k_s, vt_s, kl, att):
    acc_s, m_s, a_s, s_s, p_s = att
    tile = p_s.shape[1]
    chunk_shift = int(math.log2(CHUNK))

    def scores(j, buf):
        for h in range(N_HEADS):
            hs = slice(h * V_HEAD_DIM, (h + 1) * V_HEAD_DIM)
            kh = k_s[kl, j, :, hs]
            for mp in range(2):
                s_s[buf, 2 * h + mp] = _dot(kh, q_s[mp, hs, :])

    def softmax(first, buf):
        for idx in range(2 * N_HEADS):
            st = s_s[buf, idx]
            if first:
                k_chunk = lax.shift_right_logical(lax.broadcasted_iota(jnp.int32, (tile, tile), 0), chunk_shift)
                q_chunk = lax.shift_right_logical(lax.broadcasted_iota(jnp.int32, (tile, tile), 1), chunk_shift)
                st = jnp.where(k_chunk <= q_chunk, st, MASK_VALUE)
                m_new = jnp.max(st, axis=0, keepdims=True)
                a_s[idx] = jnp.ones((1, tile), F32)
            else:
                m_old = m_s[idx]
                m_new = jnp.maximum(m_old, jnp.max(st, axis=0, keepdims=True))
                a_s[idx] = jnp.exp2(m_old - m_new)
            m_s[idx] = m_new
            p_s[idx] = jnp.exp2(st - m_new).astype(BF16)

    def numerators(j):
        for h in range(N_HEADS):
            vth = vt_s[kl, j, h * VT_HEAD_ROWS:(h + 1) * VT_HEAD_ROWS, :]
            for mp in range(2):
                idx = 2 * h + mp
                acc_s[idx] = a_s[idx] * acc_s[idx] + _dot(vth, p_s[idx])

    last_past = jnp.maximum(tile_idx - 1, 0)
    acc_s[...] = jnp.zeros(acc_s.shape, F32)
    scores(tile_idx, 0)
    scores(0, 1)
    softmax(True, 0)

    def stage(j, buf):
        numerators(jnp.where(j == 0, tile_idx, j - 1))
        scores(jnp.minimum(j + 1, last_past), buf)
        softmax(False, 1 - buf)

    def body(jj, carry):
        stage(2 * jj, 0)
        stage(2 * jj + 1, 1)
        return carry

    lax.fori_loop(0, lax.shift_right_logical(tile_idx, 1), body, 0)

    @pl.when((tile_idx & 1) == 1)
    def _():
        stage(tile_idx - 1, 0)

    numerators(jnp.where(tile_idx == 0, tile_idx, last_past))


def _prompt_pre(x, l, i, mod_ref, prm, cos8, sin8, put_k, put_v, scr):
    k_s, vt_s, u_s, sh_s, q_s = scr[:5]
    rows = x.shape[0]
    shift, scale, gate = mod_ref[0, 0, 0:1, :], mod_ref[0, 0, 1:2, :], mod_ref[0, 0, 2:3, :]
    hb = _modulated_norm(x, prm["g_pre"][l], shift, scale)
    w_t, w_r = prm["w_t"], prm["w_r"]
    conv_out = _conv_branch(hb, w_r, l, prm, u_s, sh_s)
    conv_state = u_s[l, rows + CONV_PAD - CONV_STATE:rows + CONV_PAD, :]
    u_s[l, 0:CONV_PAD, :] = u_s[l, rows:rows + CONV_PAD, :]

    qt = _rope_t(_dot_nt(w_t[l, WT_Q:WT_Q + QK_WIDTH, :], hb), cos8, sin8) * (QK_HEAD_DIM ** -0.5 * LOG2E)
    zero = jnp.zeros((QK_HEAD_DIM, rows), F32)
    q0, q1 = [], []
    for mb in range(QK_WIDTH // QK_HEAD_DIM):
        blk = qt[mb * QK_HEAD_DIM:(mb + 1) * QK_HEAD_DIM]
        q0.append(blk if mb % 2 == 0 else zero)
        q1.append(zero if mb % 2 == 0 else blk)
    q_s[l, 0] = jnp.concatenate(q0, axis=0).astype(BF16)
    q_s[l, 1] = jnp.concatenate(q1, axis=0).astype(BF16)

    kt = _rope_t(_dot_nt(w_t[l, WT_K:WT_K + QK_WIDTH, :], hb), cos8, sin8)
    put_k(kt)
    k_s[l, i] = kt.T.astype(BF16)
    v = _dot(hb, w_r[l, :, WR_V:WR_V + ATTN_WIDTH])
    put_v(v)
    vt = v.T
    ones = jnp.ones((VT_HEAD_ROWS - V_HEAD_DIM, rows), F32)
    vt_s[l, i] = jnp.concatenate(
        [blk for h in range(N_HEADS) for blk in (vt[h * V_HEAD_DIM:(h + 1) * V_HEAD_DIM], ones)], axis=0).astype(BF16)
    return x, gate, hb, conv_out, conv_state


def _prompt_post(state, l, prm, acc_s):
    x, gate, hb, conv_out, _ = state
    lam_init = _lam_init(l)
    lam = _lam(prm["lam"][l], lam_init)
    gat = _dot_nt(prm["w_t"][l, WT_GA:WT_GA + ATTN_WIDTH, :], hb)
    g_sub = prm["g_subln_t"][l]
    heads = []
    for hd in range(N_HEADS):
        hs = slice(hd * V_HEAD_DIM, (hd + 1) * V_HEAD_DIM)
        n0, n1 = acc_s[l, 2 * hd], acc_s[l, 2 * hd + 1]
        l0, l1 = n0[V_HEAD_DIM:V_HEAD_DIM + 1], n1[V_HEAD_DIM:V_HEAD_DIM + 1]
        o = n0[0:V_HEAD_DIM] * (1.0 / l0) - n1[0:V_HEAD_DIM] * (lam / l1)
        o = o * lax.rsqrt(jnp.mean(o * o, axis=0, keepdims=True) + SUBLN_EPS) * g_sub
        heads.append((o * (1.0 - lam_init)) * _silu(gat[hs]))
    attn_out = jnp.concatenate(heads, axis=0).T.astype(BF16)
    return _out_proj(x, attn_out, conv_out, prm["w_out"], l, prm["g_post"][l], gate)


_PROMPT_PRM = ("g_pre", "g_post", "w_t", "w_r", "w_out", "lam", "g_subln_t", "w_dw", "b_dw", "g_ln", "b_ln")


def _prompt_kernel(n_tiles, n_flat, x_ref, mod0_ref, mod1_ref, cos0_ref, sin0_ref, cos1_ref, sin1_ref, *rest):
    n_prm = len(_PROMPT_PRM)
    prm = dict(zip(_PROMPT_PRM, rest[:n_prm]))
    y_ref, k_ref, v_ref, conv_ref = rest[n_prm:n_prm + 4]
    scr = rest[n_prm + 4:]
    k_s, vt_s, u_s, sh_s, q_s, acc_s, m_s, a_s, s_s, p_s, y0_s, kt_hold, v_hold, cs_hold = scr
    s = pl.program_id(0)
    i0 = lax.rem(jnp.minimum(s, n_flat - 1), n_tiles)
    i1 = lax.rem(jnp.maximum(s - 1, 0), n_tiles)
    tile = x_ref.shape[1]

    @pl.when(s == 0)
    def _():
        y0_s[...] = jnp.zeros(y0_s.shape, F32)
        kt_hold[...] = jnp.zeros(kt_hold.shape, F32)
        v_hold[...] = jnp.zeros(v_hold.shape, F32)
        cs_hold[...] = jnp.zeros(cs_hold.shape, F32)

    @pl.when(i0 == 0)
    def _():
        u_s[0, 0:CONV_PAD, :] = jnp.zeros((CONV_PAD, CONV_WIDTH), F32)

    @pl.when(i1 == 0)
    def _():
        u_s[1, 0:CONV_PAD, :] = jnp.zeros((CONV_PAD, CONV_WIDTH), F32)

    def store_v(layer, v):
        for h in range(N_HEADS):
            v_ref[layer, 0, pl.ds(h, tile, stride=N_HEADS), :] = v[:, h * V_HEAD_DIM:(h + 1) * V_HEAD_DIM]

    k_ref[0, 0] = kt_hold[...]
    store_v(0, v_hold[...])
    conv_ref[0, 0] = cs_hold[...]

    def hold_k(kt):
        kt_hold[...] = kt

    def hold_v(v):
        v_hold[...] = v

    def out_k(kt):
        k_ref[1, 0] = kt

    st0 = _prompt_pre(x_ref[0], 0, i0, mod0_ref, prm, cos0_ref[...], sin0_ref[...], hold_k, hold_v, scr)
    cs_hold[...] = st0[4]
    st1 = _prompt_pre(y0_s[...], 1, i1, mod1_ref, prm, cos1_ref[...], sin1_ref[...], out_k,
                      functools.partial(store_v, 1), scr)
    conv_ref[1, 0] = st1[4]

    _attention_t(q_s.at[0], i0, k_s, vt_s, 0, (acc_s.at[0], m_s, a_s, s_s, p_s))
    _attention_t(q_s.at[1], i1, k_s, vt_s, 1, (acc_s.at[1], m_s, a_s, s_s, p_s))

    y0_s[...] = _prompt_post(st0, 0, prm, acc_s)
    y_ref[0] = _prompt_post(st1, 1, prm, acc_s)


def _rope(z, cos_t, sa_t, sb_t):
    blocks = []
    for c in range(z.shape[1] // LANES):
        zc = z[:, c * LANES:(c + 1) * LANES]
        up = pltpu.roll(zc, LANES - ROT_HALF, 1)
        down = pltpu.roll(zc, ROT_HALF, 1)
        blocks.append(zc * cos_t + up * sa_t + down * sb_t)
    return jnp.concatenate(blocks, axis=1)


def _attention(q_s, k_own, v_own, n_past, kt_s, v_s, att):
    acc_s, m_s, l_s, a_s, s_s, p_s = att
    for h in range(N_HEADS):
        hs = slice(h * V_HEAD_DIM, (h + 1) * V_HEAD_DIM)
        for mp in range(2):
            idx = 2 * h + mp
            s = _dot_nt(q_s[mp, :, hs], k_own[:, hs])
            mx = jnp.max(s, axis=1, keepdims=True)
            p = jnp.exp(s - mx)
            m_s[idx] = mx
            l_s[idx] = jnp.sum(p, axis=1, keepdims=True)
            acc_s[idx] = _dot(p.astype(BF16), v_own[:, hs])

    for j in range(n_past):
        buf = j % 2
        for h in range(N_HEADS):
            hs = slice(h * V_HEAD_DIM, (h + 1) * V_HEAD_DIM)
            kth = kt_s[j, hs, :]
            for mp in range(2):
                s_s[buf, 2 * h + mp] = _dot(q_s[mp, :, hs], kth)
        for idx in range(2 * N_HEADS):
            s = s_s[buf, idx]
            m_old = m_s[idx]
            m_new = jnp.maximum(m_old, jnp.max(s, axis=1, keepdims=True))
            alpha = jnp.exp(m_old - m_new)
            p = jnp.exp(s - m_new)
            l_s[idx] = alpha * l_s[idx] + jnp.sum(p, axis=1, keepdims=True)
            a_s[buf, idx] = alpha
            m_s[idx] = m_new
            p_s[buf, idx] = p.astype(BF16)
        for h in range(N_HEADS):
            hs = slice(h * V_HEAD_DIM, (h + 1) * V_HEAD_DIM)
            vt = v_s[j * KEY_TILE:(j + 1) * KEY_TILE, hs]
            for mp in range(2):
                idx = 2 * h + mp
                acc_s[idx] = a_s[buf, idx] * acc_s[idx] + _dot(p_s[buf, idx], vt)


_SAMPLE_PRM = ("g_pre", "g_post", "w_t", "w_r", "w_out", "lam", "g_subln", "w_dw", "b_dw", "g_ln", "b_ln")


def _sample_kernel(depth, rows, past_len, x_ref, mod_ref, cos_ref, sa_ref, sb_ref, ck_ref, cv_ref, cs_ref, *rest):
    n_prm = len(_SAMPLE_PRM)
    prm = dict(zip(_SAMPLE_PRM, rest[:n_prm]))
    y_ref, k_ref, v_ref, conv_ref = rest[n_prm:n_prm + 4]
    kt_s, v_s, u_s, sh_s, q_s, acc_s, m_s, l_s, a_s, s_s, p_s, xin_s, y1_s = rest[n_prm + 4:]
    l = pl.program_id(0)
    b = pl.program_id(1)

    @pl.when(l == 0)
    def _():
        xin_s[...] = x_ref[0]

    @pl.when(l != 0)
    def _():
        xin_s[...] = y1_s[b]

    for j in range(past_len // KEY_TILE):
        kt_s[j] = ck_ref[0, 0, :, j * KEY_TILE:(j + 1) * KEY_TILE].astype(BF16)
    for h in range(N_HEADS):
        v_s[:, h * V_HEAD_DIM:(h + 1) * V_HEAD_DIM] = cv_ref[0, 0, pl.ds(h, past_len, stride=N_HEADS), :].astype(BF16)
    u_s[0, 0:CONV_PAD, :] = jnp.zeros((CONV_PAD, CONV_WIDTH), F32)
    u_s[0, CONV_PAD - CONV_STATE:CONV_PAD, :] = cs_ref[0, 0]

    x = xin_s[...]
    lam_init = jnp.where(l == 0, _lam_init(0), _lam_init(1)).astype(F32)
    shift, scale, gate = mod_ref[0, 0, 0:1, :], mod_ref[0, 0, 1:2, :], mod_ref[0, 0, 2:3, :]
    hb = _modulated_norm(x, prm["g_pre"][0], shift, scale)
    w_t, w_r = prm["w_t"], prm["w_r"]
    tabs = (cos_ref[...], sa_ref[...], sb_ref[...])

    q = _rope(_dot_nt(hb, w_t[0, WT_Q:WT_Q + QK_WIDTH, :]), *tabs) * (QK_HEAD_DIM ** -0.5)
    first_map = (lax.broadcasted_iota(jnp.int32, (rows, QK_WIDTH), 1) & (V_HEAD_DIM - 1)) < QK_HEAD_DIM
    q_s[0] = jnp.where(first_map, q, 0.0).astype(BF16)
    q_s[1] = jnp.where(first_map, 0.0, q).astype(BF16)
    k = _rope(_dot_nt(hb, w_t[0, WT_K:WT_K + QK_WIDTH, :]), *tabs)
    v = _dot(hb, w_r[0, :, WR_V:WR_V + ATTN_WIDTH])
    k_ref[0, 0] = k
    v_ref[0, 0] = v
    _attention(q_s, k.astype(BF16), v.astype(BF16), past_len // KEY_TILE, kt_s, v_s,
               (acc_s, m_s, l_s, a_s, s_s, p_s))

    lam = _lam(prm["lam"][0], lam_init)
    ga = _dot_nt(hb, w_t[0, WT_GA:WT_GA + ATTN_WIDTH, :])
    g_sub = prm["g_subln"][0]
    heads = []
    for hd in range(N_HEADS):
        hs = slice(hd * V_HEAD_DIM, (hd + 1) * V_HEAD_DIM)
        o = acc_s[2 * hd] * (1.0 / l_s[2 * hd]) - acc_s[2 * hd + 1] * (lam / l_s[2 * hd + 1])
        o = o * lax.rsqrt(jnp.mean(o * o, axis=-1, keepdims=True) + SUBLN_EPS) * g_sub
        heads.append(((o * (1.0 - lam_init)) * _silu(ga[:, hs])).astype(BF16))
    attn_out = jnp.concatenate(heads, axis=1)

    conv_out = _conv_branch(hb, w_r, 0, prm, u_s, sh_s)
    y = _out_proj(x, attn_out, conv_out, prm["w_out"], 0, prm["g_post"][0], gate)
    conv_ref[0, 0] = u_s[0, rows + CONV_PAD - CONV_STATE:rows + CONV_PAD, :]
    y1_s[b] = y

    @pl.when(l == depth - 1)
    def _():
        y_ref[0] = y


def _mod_kernel(c_ref, w_ref, b_ref, o_ref):
    c = c_ref[...]
    o_ref[0] = _dot(_silu(c), w_ref[0]) + b_ref[0]


def _rope_angles(pos):
    inv = jnp.float32(ROPE_THETA) ** (-jnp.arange(0, ROT_DIM, 2, dtype=F32) / ROT_DIM)
    return pos.astype(F32)[:, None] * inv[None, :]


def _rope_tables_rowmajor(pos):
    ang = _rope_angles(pos)
    lane = jnp.arange(LANES, dtype=jnp.int32) % QK_HEAD_DIM
    ang = ang[:, lane % ROT_HALF]
    cos, sin = jnp.cos(ang), jnp.sin(ang)
    lo = (lane < ROT_HALF)[None, :]
    hi = ((lane >= ROT_HALF) & (lane < ROT_DIM))[None, :]
    return (jnp.where(lo | hi, cos, 1.0), jnp.where(lo, -sin, 0.0), jnp.where(hi, sin, 0.0))


def _const_spec(shape):
    nd = len(shape)
    return pl.BlockSpec(shape, lambda *_: (0,) * nd, pipeline_mode=pl.Buffered(1))


def kernel(x_prompt, x_sample, c_prompt, c_sample, cache_k, cache_v, state_conv, w_ada, b_ada, g_pre, g_post, w_in,
           w_out, lam_q1, lam_k1, lam_q2, lam_k2, g_subln, w_dw, b_dw, g_ln, b_ln):
    batch, seq, _ = x_prompt.shape
    dec_batch, dec_seq, _ = x_sample.shape
    depth = w_in.shape[0]
    past_len = cache_k.shape[2]
    tile = min(PROMPT_TILE, seq)
    n_tiles = seq // tile
    assert seq % tile == 0 and tile % LANES == 0 and past_len % KEY_TILE == 0 and dec_seq % SUBLANES == 0

    c_all = jnp.concatenate([c_prompt, c_sample], axis=0)
    n_c = c_all.shape[0]
    mod_cols = 512
    mod = pl.pallas_call(
        _mod_kernel,
        grid=(depth, 3 * D_MODEL // mod_cols),
        in_specs=[pl.BlockSpec((n_c, D_MODEL), lambda l, n: (0, 0)),
                  pl.BlockSpec((1, D_MODEL, mod_cols), lambda l, n: (l, 0, n)),
                  pl.BlockSpec((1, 1, mod_cols), lambda l, n: (l, 0, n))],
        out_specs=pl.BlockSpec((1, n_c, mod_cols), lambda l, n: (l, 0, n)),
        out_shape=jax.ShapeDtypeStruct((depth, n_c, 3 * D_MODEL), F32),
        name="adaln_mod",
    )(c_all, w_ada, b_ada.reshape(depth, 1, 3 * D_MODEL))
    mod_p = mod[:, :batch].reshape(depth, batch, 3, D_MODEL)
    mod_s = mod[:, batch:].reshape(depth, dec_batch, 3, D_MODEL)

    w_in_b = w_in.astype(BF16)
    w_t = jnp.swapaxes(jnp.concatenate([w_in_b[:, :, COL_Q:COL_V], w_in_b[:, :, COL_GA:COL_GLU_A]], axis=2), 1, 2)
    w_r = jnp.concatenate([w_in_b[:, :, COL_V:COL_GA], w_in_b[:, :, COL_GLU_A:IN_COLS]], axis=2)
    w_dw_p = jnp.concatenate([w_dw, jnp.zeros((depth, CONV_PAD - CONV_KERNEL, CONV_WIDTH), w_dw.dtype)], axis=1)
    shared = {
        "g_pre": g_pre.reshape(depth, 1, D_MODEL), "g_post": g_post.reshape(depth, 1, D_MODEL),
        "w_t": w_t, "w_r": w_r, "w_out": w_out.astype(BF16),
        "lam": jnp.stack([lam_q1, lam_k1, lam_q2, lam_k2], axis=1),
        "g_subln": g_subln.reshape(depth, 1, V_HEAD_DIM), "g_subln_t": g_subln.reshape(depth, V_HEAD_DIM, 1),
        "w_dw": w_dw_p, "b_dw": b_dw.reshape(depth, 1, CONV_WIDTH),
        "g_ln": g_ln.reshape(depth, 1, CONV_WIDTH), "b_ln": b_ln.reshape(depth, 1, CONV_WIDTH)}

    assert depth == 2, "the prompt call interleaves exactly two layers"
    n_flat = batch * n_tiles
    ang = _rope_angles(jnp.arange(seq, dtype=jnp.int32)).T
    cos_p, sin_p = jnp.cos(ang), jnp.sin(ang)
    params_p = [shared[n] for n in _PROMPT_PRM]

    def tile0(s):
        f = jnp.minimum(s, n_flat - 1)
        return f // n_tiles, f % n_tiles

    def tile1(s):
        f = jnp.maximum(s - 1, 0)
        return f // n_tiles, f % n_tiles

    mod_block = (1, 1, 3, D_MODEL)
    tab_block = (ROT_HALF, tile)
    yp, kp_t, vp, cp = pl.pallas_call(
        functools.partial(_prompt_kernel, n_tiles, n_flat),
        grid=(n_flat + 1,),
        in_specs=[pl.BlockSpec((1, tile, D_MODEL), lambda s: tile0(s) + (0,)),
                  pl.BlockSpec(mod_block, lambda s: (0, tile0(s)[0], 0, 0)),
                  pl.BlockSpec(mod_block, lambda s: (1, tile1(s)[0], 0, 0)),
                  pl.BlockSpec(tab_block, lambda s: (0, tile0(s)[1])),
                  pl.BlockSpec(tab_block, lambda s: (0, tile0(s)[1])),
                  pl.BlockSpec(tab_block, lambda s: (0, tile1(s)[1])),
                  pl.BlockSpec(tab_block, lambda s: (0, tile1(s)[1]))]
                 + [_const_spec(p.shape) for p in params_p],
        out_specs=[pl.BlockSpec((1, tile, D_MODEL), lambda s: tile1(s) + (0,)),
                   pl.BlockSpec((depth, 1, QK_WIDTH, tile), lambda s: (0, tile1(s)[0], 0, tile1(s)[1])),
                   pl.BlockSpec((depth, 1, tile * N_HEADS, V_HEAD_DIM), lambda s: (0,) + tile1(s) + (0,)),
                   pl.BlockSpec((depth, 1, CONV_STATE, CONV_WIDTH), lambda s: (0, tile1(s)[0], 0, 0))],
        out_shape=[jax.ShapeDtypeStruct((batch, seq, D_MODEL), F32),
                   jax.ShapeDtypeStruct((depth, batch, QK_WIDTH, seq), F32),
                   jax.ShapeDtypeStruct((depth, batch, seq * N_HEADS, V_HEAD_DIM), F32),
                   jax.ShapeDtypeStruct((depth, batch, CONV_STATE, CONV_WIDTH), F32)],
        scratch_shapes=[pltpu.VMEM((depth, n_tiles, tile, QK_WIDTH), BF16),
                        pltpu.VMEM((depth, n_tiles, N_HEADS * VT_HEAD_ROWS, tile), BF16),
                        pltpu.VMEM((depth, CONV_PAD + tile, CONV_WIDTH), F32),
                        pltpu.VMEM((2, CONV_WIDTH // LANES, CONV_PAD + tile, LANES), F32),
                        pltpu.VMEM((depth, 2, QK_WIDTH, tile), BF16),
                        pltpu.VMEM((depth, 2 * N_HEADS, VT_HEAD_ROWS, tile), F32),
                        pltpu.VMEM((2 * N_HEADS, 1, tile), F32),
                        pltpu.VMEM((2 * N_HEADS, 1, tile), F32),
                        pltpu.VMEM((2, 2 * N_HEADS, tile, tile), F32),
                        pltpu.VMEM((2 * N_HEADS, tile, tile), BF16),
                        pltpu.VMEM((tile, D_MODEL), F32),
                        pltpu.VMEM((QK_WIDTH, tile), F32),
                        pltpu.VMEM((tile, ATTN_WIDTH), F32),
                        pltpu.VMEM((CONV_STATE, CONV_WIDTH), F32)],
        compiler_params=pltpu.CompilerParams(dimension_semantics=("arbitrary",),
                                             vmem_limit_bytes=VMEM_LIMIT_BYTES),
        name="prompt_layers",
    )(x_prompt, mod_p, mod_p, cos_p, sin_p, cos_p, sin_p, *params_p)
    kp = jnp.transpose(kp_t.reshape(depth, batch, N_HEADS, 2, QK_HEAD_DIM, seq), (0, 1, 5, 2, 3, 4))

    tabs_s = _rope_tables_rowmajor(past_len + jnp.arange(dec_seq, dtype=jnp.int32))
    params_s = [shared[n] for n in _SAMPLE_PRM]

    def layer_spec(shape):
        nd = len(shape)
        return pl.BlockSpec((1,) + tuple(shape[1:]), lambda l, b: (l,) + (0,) * (nd - 1))

    lb_spec = lambda shape: pl.BlockSpec((1, 1) + tuple(shape[2:]), lambda l, b: (l, b) + (0,) * (len(shape) - 2))
    ck_t = jnp.transpose(cache_k, (0, 1, 3, 4, 5, 2)).reshape(depth, dec_batch, QK_WIDTH, past_len)
    cv = cache_v.reshape(depth, dec_batch, past_len * N_HEADS, V_HEAD_DIM)
    row_spec = pl.BlockSpec((1, dec_seq, D_MODEL), lambda l, b: (b, 0, 0))
    stab_spec = pl.BlockSpec((dec_seq, LANES), lambda l, b: (0, 0))
    ys, ks, vs, cs = pl.pallas_call(
        functools.partial(_sample_kernel, depth, dec_seq, past_len),
        grid=(depth, dec_batch),
        in_specs=[row_spec, lb_spec(mod_s.shape), stab_spec, stab_spec, stab_spec,
                  lb_spec(ck_t.shape), lb_spec(cv.shape), lb_spec(state_conv.shape)]
                 + [layer_spec(p.shape) for p in params_s],
        out_specs=[pl.BlockSpec((1, dec_seq, D_MODEL), lambda l, b: (jnp.where(l == depth - 1, b, 0), 0, 0)),
                   lb_spec((depth, dec_batch, dec_seq, QK_WIDTH)),
                   lb_spec((depth, dec_batch, dec_seq, ATTN_WIDTH)), lb_spec(state_conv.shape)],
        out_shape=[jax.ShapeDtypeStruct((dec_batch, dec_seq, D_MODEL), F32),
                   jax.ShapeDtypeStruct((depth, dec_batch, dec_seq, QK_WIDTH), F32),
                   jax.ShapeDtypeStruct((depth, dec_batch, dec_seq, ATTN_WIDTH), F32),
                   jax.ShapeDtypeStruct(state_conv.shape, F32)],
        scratch_shapes=[pltpu.VMEM((past_len // KEY_TILE, QK_WIDTH, KEY_TILE), BF16),
                        pltpu.VMEM((past_len, ATTN_WIDTH), BF16),
                        pltpu.VMEM((1, CONV_PAD + dec_seq, CONV_WIDTH), F32),
                        pltpu.VMEM((2, CONV_WIDTH // LANES, CONV_PAD + dec_seq, LANES), F32),
                        pltpu.VMEM((2, dec_seq, QK_WIDTH), BF16),
                        pltpu.VMEM((2 * N_HEADS, dec_seq, V_HEAD_DIM), F32),
                        pltpu.VMEM((2 * N_HEADS, dec_seq, 1), F32), pltpu.VMEM((2 * N_HEADS, dec_seq, 1), F32),
                        pltpu.VMEM((2, 2 * N_HEADS, dec_seq, 1), F32),
                        pltpu.VMEM((2, 2 * N_HEADS, dec_seq, KEY_TILE), F32),
                        pltpu.VMEM((2, 2 * N_HEADS, dec_seq, KEY_TILE), BF16),
                        pltpu.VMEM((dec_seq, D_MODEL), F32), pltpu.VMEM((dec_batch, dec_seq, D_MODEL), F32)],
        compiler_params=pltpu.CompilerParams(dimension_semantics=("arbitrary", "arbitrary"),
                                             vmem_limit_bytes=VMEM_LIMIT_BYTES),
        name="sample_layers",
    )(x_sample, mod_s, *tabs_s, ck_t, cv, state_conv, *params_s)

    return (yp, ys, kp, vp.reshape(depth, batch, seq, N_HEADS, V_HEAD_DIM), cp,
            ks.reshape(depth, dec_batch, dec_seq, N_HEADS, 2, QK_HEAD_DIM),
            vs.reshape(depth, dec_batch, dec_seq, N_HEADS, V_HEAD_DIM), cs)
```

```python
import functools
import math

import jax
import jax.numpy as jnp
from jax import lax
from jax.experimental import pallas as pl
from jax.experimental.pallas import tpu as pltpu

F32 = jnp.float32
BF16 = jnp.bfloat16

D_MODEL = 1024
N_HEADS = 4
QK_HEAD_DIM = 64
V_HEAD_DIM = 128
ATTN_WIDTH = N_HEADS * V_HEAD_DIM
QK_WIDTH = N_HEADS * 2 * QK_HEAD_DIM
CONV_WIDTH = D_MODEL - ATTN_WIDTH
CONV_KERNEL = 31
CONV_STATE = CONV_KERNEL - 1
CHUNK = 64
ROT_DIM = 16
ROT_HALF = ROT_DIM // 2
ROPE_THETA = 500000.0
NORM_EPS = 1e-6
SUBLN_EPS = 1e-5
LN_EPS = 1e-5
COL_Q, COL_K, COL_V, COL_GA = 0, QK_WIDTH, 2 * QK_WIDTH, 2 * QK_WIDTH + ATTN_WIDTH
COL_GLU_A = 2 * QK_WIDTH + 2 * ATTN_WIDTH
COL_GLU_B = COL_GLU_A + CONV_WIDTH
COL_GC = COL_GLU_B + CONV_WIDTH
IN_COLS = COL_GC + CONV_WIDTH
WT_Q, WT_K, WT_GA = 0, QK_WIDTH, 2 * QK_WIDTH
WR_V, WR_GLU_A, WR_GLU_B, WR_GC = 0, ATTN_WIDTH, ATTN_WIDTH + CONV_WIDTH, ATTN_WIDTH + 2 * CONV_WIDTH

LANES = 128
SUBLANES = 8
CONV_PAD = 32
MASK_VALUE = -1e30
LOG2E = math.log2(math.e)
PROMPT_TILE = 256
KEY_TILE = 2048
VMEM_LIMIT_BYTES = 60000 * 1024
VT_HEAD_ROWS = V_HEAD_DIM + 16


def _silu(x):
    return x * jax.nn.sigmoid(x)


def _dot(a, b):
    return jnp.dot(a, b, preferred_element_type=F32)


def _dot_nt(a, b):
    return lax.dot_general(a, b, (((1,), (1,)), ((), ())), preferred_element_type=F32)


def _lam_init(layer_idx):
    return 0.8 - 0.6 * math.exp(-0.3 * layer_idx)


def _lam(lam_p, lam_init):
    return (jnp.exp(jnp.sum(lam_p[0:1] * lam_p[1:2], axis=1, keepdims=True))
            - jnp.exp(jnp.sum(lam_p[2:3] * lam_p[3:4], axis=1, keepdims=True)) + lam_init)


def _modulated_norm(x, g_pre, shift, scale):
    ms = jnp.mean(x * x, axis=-1, keepdims=True)
    return ((x * lax.rsqrt(ms + NORM_EPS)) * (g_pre * (1.0 + scale)) + shift).astype(BF16)


def _conv_taps(u_s, sl, sh_s, rows, wdw_ref, bdw):
    total = rows + CONV_PAD
    pitch = total // SUBLANES
    assert total % SUBLANES == 0 and pitch % SUBLANES != 0
    n_blocks = CONV_WIDTH // LANES
    for c in range(n_blocks):
        sh_s[0, c] = u_s[sl, :, c * LANES:(c + 1) * LANES]
    for c in range(n_blocks):
        lanes = slice(c * LANES, (c + 1) * LANES)

        def rows_of(j, c=c):
            return sh_s[0, c, pl.ds(j, SUBLANES, stride=pitch), :]

        moved = {}
        for j in range(-CONV_STATE, 0):
            wraps = -(j // pitch)
            moved[j] = pltpu.roll(rows_of(j + wraps * pitch), wraps, 0)
        accs = [jnp.broadcast_to(bdw[:, lanes], (SUBLANES, LANES))] * pitch
        for d in range(-CONV_STATE, 1):
            w_d = jnp.broadcast_to(wdw_ref[sl, d + CONV_STATE:d + CONV_STATE + 1, lanes], (SUBLANES, LANES))
            accs = [acc + w_d * (rows_of(k + d) if k + d >= 0 else moved[k + d]) for k, acc in enumerate(accs)]
        for k, acc in enumerate(accs):
            sh_s[1, c, pl.ds(k, SUBLANES, stride=pitch), :] = acc
    return jnp.concatenate([sh_s[1, c, CONV_PAD:CONV_PAD + rows, :] for c in range(n_blocks)], axis=1)


def _conv_branch(hb, w_r, sl, prm, u_s, sh_s):
    rows = hb.shape[0]
    u = (_dot(hb, w_r[sl, :, WR_GLU_A:WR_GLU_A + CONV_WIDTH])
         * jax.nn.sigmoid(_dot(hb, w_r[sl, :, WR_GLU_B:WR_GLU_B + CONV_WIDTH])))
    u_s[sl, CONV_PAD:CONV_PAD + rows, :] = u
    conv = _conv_taps(u_s, sl, sh_s, rows, prm["w_dw"], prm["b_dw"][sl])
    mu = jnp.mean(conv, axis=-1, keepdims=True)
    xc = conv - mu
    ln = xc * lax.rsqrt(jnp.mean(xc * xc, axis=-1, keepdims=True) + LN_EPS) * prm["g_ln"][sl] + prm["b_ln"][sl]
    gc = _dot(hb, w_r[sl, :, WR_GC:WR_GC + CONV_WIDTH])
    return (_silu(ln) * _silu(gc)).astype(BF16)


def _out_proj(x, attn_out, conv_out, w_out, sl, g_post, gate):
    m = _dot(attn_out, w_out[sl, 0:ATTN_WIDTH, :]) + _dot(conv_out, w_out[sl, ATTN_WIDTH:ATTN_WIDTH + CONV_WIDTH, :])
    mn = m * lax.rsqrt(jnp.mean(m * m, axis=-1, keepdims=True) + NORM_EPS) * g_post
    return x + gate * mn


def _rope_t(zt, cos8, sin8):
    parts = []
    for mb in range(zt.shape[0] // QK_HEAD_DIM):
        b0 = mb * QK_HEAD_DIM
        x1, x2 = zt[b0:b0 + ROT_HALF], zt[b0 + ROT_HALF:b0 + ROT_DIM]
        parts += [x1 * cos8 - x2 * sin8, x2 * cos8 + x1 * sin8, zt[b0 + ROT_DIM:b0 + QK_HEAD_DIM]]
    return jnp.concatenate(parts, axis=0)


def _attention_t(q_s, tile_idx, k_s, vt_s, kl, att):
    acc_s, m_s, a_s, s_s, p_s = att
    tile = p_s.shape[1]
    chunk_shift = int(math.log2(CHUNK))

    def scores(j, buf):
        for h in range(N_HEADS):
            hs = slice(h * V_HEAD_DIM, (h + 1) * V_HEAD_DIM)
            kh = k_s[kl, j, :, hs]
            for mp in range(2):
                s_s[buf, 2 * h + mp] = _dot(kh, q_s[mp, hs, :])

    def softmax(first, buf):
        for idx in range(2 * N_HEADS):
            st = s_s[buf, idx]
            if first:
                k_chunk = lax.shift_right_logical(lax.broadcasted_iota(jnp.int32, (tile, tile), 0), chunk_shift)
                q_chunk = lax.shift_right_logical(lax.broadcasted_iota(jnp.int32, (tile, tile), 1), chunk_shift)
                st = jnp.where(k_chunk <= q_chunk, st, MASK_VALUE)
                m_new = jnp.max(st, axis=0, keepdims=True)
                a_s[idx] = jnp.ones((1, tile), F32)
            else:
                m_old = m_s[idx]
                m_new = jnp.maximum(m_old, jnp.max(st, axis=0, keepdims=True))
                a_s[idx] = jnp.exp2(m_old - m_new)
            m_s[idx] = m_new
            p_s[idx] = jnp.exp2(st - m_new).astype(BF16)

    def numerators(j):
        for h in range(N_HEADS):
            vth = vt_s[kl, j, h * VT_HEAD_ROWS:(h + 1) * VT_HEAD_ROWS, :]
            for mp in range(2):
                idx = 2 * h + mp
                acc_s[idx] = a_s[idx] * acc_s[idx] + _dot(vth, p_s[idx])

    last_past = jnp.maximum(tile_idx - 1, 0)
    acc_s[...] = jnp.zeros(acc_s.shape, F32)
    scores(tile_idx, 0)
    scores(0, 1)
    softmax(True, 0)

    def stage(j, buf):
        numerators(jnp.where(j == 0, tile_idx, j - 1))
        scores(jnp.minimum(j + 1, last_past), buf)
        softmax(False, 1 - buf)

    def body(jj, carry):
        stage(2 * jj, 0)
        stage(2 * jj + 1, 1)
        return carry

    lax.fori_loop(0, lax.shift_right_logical(tile_idx, 1), body, 0)

    @pl.when((tile_idx & 1) == 1)
    def _():
        stage(tile_idx - 1, 0)

    numerators(jnp.where(tile_idx == 0, tile_idx, last_past))


def _prompt_pre(x, l, i, mod_ref, prm, cos8, sin8, put_k, put_v, scr):
    k_s, vt_s, u_s, sh_s, q_s = scr[:5]
    rows = x.shape[0]
    shift, scale, gate = mod_ref[0, 0, 0:1, :], mod_ref[0, 0, 1:2, :], mod_ref[0, 0, 2:3, :]
    hb = _modulated_norm(x, prm["g_pre"][l], shift, scale)
    w_t, w_r = prm["w_t"], prm["w_r"]
    conv_out = _conv_branch(hb, w_r, l, prm, u_s, sh_s)
    conv_state = u_s[l, rows + CONV_PAD - CONV_STATE:rows + CONV_PAD, :]
    u_s[l, 0:CONV_PAD, :] = u_s[l, rows:rows + CONV_PAD, :]

    qt = _rope_t(_dot_nt(w_t[l, WT_Q:WT_Q + QK_WIDTH, :], hb), cos8, sin8) * (QK_HEAD_DIM ** -0.5 * LOG2E)
    zero = jnp.zeros((QK_HEAD_DIM, rows), F32)
    q0, q1 = [], []
    for mb in range(QK_WIDTH // QK_HEAD_DIM):
        blk = qt[mb * QK_HEAD_DIM:(mb + 1) * QK_HEAD_DIM]
        q0.append(blk if mb % 2 == 0 else zero)
        q1.append(zero if mb % 2 == 0 else blk)
    q_s[l, 0] = jnp.concatenate(q0, axis=0).astype(BF16)
    q_s[l, 1] = jnp.concatenate(q1, axis=0).astype(BF16)

    kt = _rope_t(_dot_nt(w_t[l, WT_K:WT_K + QK_WIDTH, :], hb), cos8, sin8)
    put_k(kt)
    k_s[l, i] = kt.T.astype(BF16)
    v = _dot(hb, w_r[l, :, WR_V:WR_V + ATTN_WIDTH])
    put_v(v)
    vt = v.T
    ones = jnp.ones((VT_HEAD_ROWS - V_HEAD_DIM, rows), F32)
    vt_s[l, i] = jnp.concatenate(
        [blk for h in range(N_HEADS) for blk in (vt[h * V_HEAD_DIM:(h + 1) * V_HEAD_DIM], ones)], axis=0).astype(BF16)
    return x, gate, hb, conv_out, conv_state


def _prompt_post(state, l, prm, acc_s):
    x, gate, hb, conv_out, _ = state
    lam_init = _lam_init(l)
    lam = _lam(prm["lam"][l], lam_init)
    gat = _dot_nt(prm["w_t"][l, WT_GA:WT_GA + ATTN_WIDTH, :], hb)
    g_sub = prm["g_subln_t"][l]
    heads = []
    for hd in range(N_HEADS):
        hs = slice(hd * V_HEAD_DIM, (hd + 1) * V_HEAD_DIM)
        n0, n1 = acc_s[l, 2 * hd], acc_s[l, 2 * hd + 1]
        l0, l1 = n0[V_HEAD_DIM:V_HEAD_DIM + 1], n1[V_HEAD_DIM:V_HEAD_DIM + 1]
        o = n0[0:V_HEAD_DIM] * (1.0 / l0) - n1[0:V_HEAD_DIM] * (lam / l1)
        o = o * lax.rsqrt(jnp.mean(o * o, axis=0, keepdims=True) + SUBLN_EPS) * g_sub
        heads.append((o * (1.0 - lam_init)) * _silu(gat[hs]))
    attn_out = jnp.concatenate(heads, axis=0).T.astype(BF16)
    return _out_proj(x, attn_out, conv_out, prm["w_out"], l, prm["g_post"][l], gate)


_PROMPT_PRM = ("g_pre", "g_post", "w_t", "w_r", "w_out", "lam", "g_subln_t", "w_dw", "b_dw", "g_ln", "b_ln")


def _prompt_kernel(n_tiles, n_flat, x_ref, mod0_ref, mod1_ref, cos0_ref, sin0_ref, cos1_ref, sin1_ref, *rest):
    n_prm = len(_PROMPT_PRM)
    prm = dict(zip(_PROMPT_PRM, rest[:n_prm]))
    y_ref, k_ref, v_ref, conv_ref = rest[n_prm:n_prm + 4]
    scr = rest[n_prm + 4:]
    k_s, vt_s, u_s, sh_s, q_s, acc_s, m_s, a_s, s_s, p_s, y0_s, kt_hold, v_hold, cs_hold = scr
    s = pl.program_id(0)
    i0 = lax.rem(jnp.minimum(s, n_flat - 1), n_tiles)
    i1 = lax.rem(jnp.maximum(s - 1, 0), n_tiles)
    tile = x_ref.shape[1]

    @pl.when(s == 0)
    def _():
        y0_s[...] = jnp.zeros(y0_s.shape, F32)
        kt_hold[...] = jnp.zeros(kt_hold.shape, F32)
        v_hold[...] = jnp.zeros(v_hold.shape, F32)
        cs_hold[...] = jnp.zeros(cs_hold.shape, F32)

    @pl.when(i0 == 0)
    def _():
        u_s[0, 0:CONV_PAD, :] = jnp.zeros((CONV_PAD, CONV_WIDTH), F32)

    @pl.when(i1 == 0)
    def _():
        u_s[1, 0:CONV_PAD, :] = jnp.zeros((CONV_PAD, CONV_WIDTH), F32)

    def store_v(layer, v):
        for h in range(N_HEADS):
            v_ref[layer, 0, pl.ds(h, tile, stride=N_HEADS), :] = v[:, h * V_HEAD_DIM:(h + 1) * V_HEAD_DIM]

    k_ref[0, 0] = kt_hold[...]
    store_v(0, v_hold[...])
    conv_ref[0, 0] = cs_hold[...]

    def hold_k(kt):
        kt_hold[...] = kt

    def hold_v(v):
        v_hold[...] = v

    def out_k(kt):
        k_ref[1, 0] = kt

    st0 = _prompt_pre(x_ref[0], 0, i0, mod0_ref, prm, cos0_ref[...], sin0_ref[...], hold_k, hold_v, scr)
    cs_hold[...] = st0[4]
    st1 = _prompt_pre(y0_s[...], 1, i1, mod1_ref, prm, cos1_ref[...], sin1_ref[...], out_k,
                      functools.partial(store_v, 1), scr)
    conv_ref[1, 0] = st1[4]

    _attention_t(q_s.at[0], i0, k_s, vt_s, 0, (acc_s.at[0], m_s, a_s, s_s, p_s))
    _attention_t(q_s.at[1], i1, k_s, vt_s, 1, (acc_s.at[1], m_s, a_s, s_s, p_s))

    y0_s[...] = _prompt_post(st0, 0, prm, acc_s)
    y_ref[0] = _prompt_post(st1, 1, prm, acc_s)


def _rope(z, cos_t, sa_t, sb_t):
    blocks = []
    for c in range(z.shape[1] // LANES):
        zc = z[:, c * LANES:(c + 1) * LANES]
        up = pltpu.roll(zc, LANES - ROT_HALF, 1)
        down = pltpu.roll(zc, ROT_HALF, 1)
        blocks.append(zc * cos_t + up * sa_t + down * sb_t)
    return jnp.concatenate(blocks, axis=1)


def _attention(q_s, k_own, v_own, n_past, kt_s, v_s, att):
    acc_s, m_s, l_s, a_s, s_s, p_s = att
    for h in range(N_HEADS):
        hs = slice(h * V_HEAD_DIM, (h + 1) * V_HEAD_DIM)
        for mp in range(2):
            idx = 2 * h + mp
            s = _dot_nt(q_s[mp, :, hs], k_own[:, hs])
            mx = jnp.max(s, axis=1, keepdims=True)
            p = jnp.exp(s - mx)
            m_s[idx] = mx
            l_s[idx] = jnp.sum(p, axis=1, keepdims=True)
            acc_s[idx] = _dot(p.astype(BF16), v_own[:, hs])

    for j in range(n_past):
        buf = j % 2
        for h in range(N_HEADS):
            hs = slice(h * V_HEAD_DIM, (h + 1) * V_HEAD_DIM)
            kth = kt_s[j, hs, :]
            for mp in range(2):
                s_s[buf, 2 * h + mp] = _dot(q_s[mp, :, hs], kth)
        for idx in range(2 * N_HEADS):
            s = s_s[buf, idx]
            m_old = m_s[idx]
            m_new = jnp.maximum(m_old, jnp.max(s, axis=1, keepdims=True))
            alpha = jnp.exp(m_old - m_new)
            p = jnp.exp(s - m_new)
            l_s[idx] = alpha * l_s[idx] + jnp.sum(p, axis=1, keepdims=True)
            a_s[buf, idx] = alpha
            m_s[idx] = m_new
            p_s[buf, idx] = p.astype(BF16)
        for h in range(N_HEADS):
            hs = slice(h * V_HEAD_DIM, (h + 1) * V_HEAD_DIM)
            vt = v_s[j * KEY_TILE:(j + 1) * KEY_TILE, hs]
            for mp in range(2):
                idx = 2 * h + mp
                acc_s[idx] = a_s[buf, idx] * acc_s[idx] + _dot(p_s[buf, idx], vt)


_SAMPLE_PRM = ("g_pre", "g_post", "w_t", "w_r", "w_out", "lam", "g_subln", "w_dw", "b_dw", "g_ln", "b_ln")


def _sample_kernel(depth, rows, past_len, x_ref, mod_ref, cos_ref, sa_ref, sb_ref, ck_ref, cv_ref, cs_ref, *rest):
    n_prm = len(_SAMPLE_PRM)
    prm = dict(zip(_SAMPLE_PRM, rest[:n_prm]))
    y_ref, k_ref, v_ref, conv_ref = rest[n_prm:n_prm + 4]
    kt_s, v_s, u_s, sh_s, q_s, acc_s, m_s, l_s, a_s, s_s, p_s, xin_s, y1_s = rest[n_prm + 4:]
    l = pl.program_id(0)
    b = pl.program_id(1)

    @pl.when(l == 0)
    def _():
        xin_s[...] = x_ref[0]

    @pl.when(l != 0)
    def _():
        xin_s[...] = y1_s[b]

    for j in range(past_len // KEY_TILE):
        kt_s[j] = ck_ref[0, 0, :, j * KEY_TILE:(j + 1) * KEY_TILE].astype(BF16)
    for h in range(N_HEADS):
        v_s[:, h * V_HEAD_DIM:(h + 1) * V_HEAD_DIM] = cv_ref[0, 0, pl.ds(h, past_len, stride=N_HEADS), :].astype(BF16)
    u_s[0, 0:CONV_PAD, :] = jnp.zeros((CONV_PAD, CONV_WIDTH), F32)
    u_s[0, CONV_PAD - CONV_STATE:CONV_PAD, :] = cs_ref[0, 0]

    x = xin_s[...]
    lam_init = jnp.where(l == 0, _lam_init(0), _lam_init(1)).astype(F32)
    shift, scale, gate = mod_ref[0, 0, 0:1, :], mod_ref[0, 0, 1:2, :], mod_ref[0, 0, 2:3, :]
    hb = _modulated_norm(x, prm["g_pre"][0], shift, scale)
    w_t, w_r = prm["w_t"], prm["w_r"]
    tabs = (cos_ref[...], sa_ref[...], sb_ref[...])

    q = _rope(_dot_nt(hb, w_t[0, WT_Q:WT_Q + QK_WIDTH, :]), *tabs) * (QK_HEAD_DIM ** -0.5)
    first_map = (lax.broadcasted_iota(jnp.int32, (rows, QK_WIDTH), 1) & (V_HEAD_DIM - 1)) < QK_HEAD_DIM
    q_s[0] = jnp.where(first_map, q, 0.0).astype(BF16)
    q_s[1] = jnp.where(first_map, 0.0, q).astype(BF16)
    k = _rope(_dot_nt(hb, w_t[0, WT_K:WT_K + QK_WIDTH, :]), *tabs)
    v = _dot(hb, w_r[0, :, WR_V:WR_V + ATTN_WIDTH])
    k_ref[0, 0] = k
    v_ref[0, 0] = v
    _attention(q_s, k.astype(BF16), v.astype(BF16), past_len // KEY_TILE, kt_s, v_s,
               (acc_s, m_s, l_s, a_s, s_s, p_s))

    lam = _lam(prm["lam"][0], lam_init)
    ga = _dot_nt(hb, w_t[0, WT_GA:WT_GA + ATTN_WIDTH, :])
    g_sub = prm["g_subln"][0]
    heads = []
    for hd in range(N_HEADS):
        hs = slice(hd * V_HEAD_DIM, (hd + 1) * V_HEAD_DIM)
        o = acc_s[2 * hd] * (1.0 / l_s[2 * hd]) - acc_s[2 * hd + 1] * (lam / l_s[2 * hd + 1])
        o = o * lax.rsqrt(jnp.mean(o * o, axis=-1, keepdims=True) + SUBLN_EPS) * g_sub
        heads.append(((o * (1.0 - lam_init)) * _silu(ga[:, hs])).astype(BF16))
    attn_out = jnp.concatenate(heads, axis=1)

    conv_out = _conv_branch(hb, w_r, 0, prm, u_s, sh_s)
    y = _out_proj(x, attn_out, conv_out, prm["w_out"], 0, prm["g_post"][0], gate)
    conv_ref[0, 0] = u_s[0, rows + CONV_PAD - CONV_STATE:rows + CONV_PAD, :]
    y1_s[b] = y

    @pl.when(l == depth - 1)
    def _():
        y_ref[0] = y


def _mod_kernel(c_ref, w_ref, b_ref, o_ref):
    c = c_ref[...]
    o_ref[0] = _dot(_silu(c), w_ref[0]) + b_ref[0]


def _rope_angles(pos):
    inv = jnp.float32(ROPE_THETA) ** (-jnp.arange(0, ROT_DIM, 2, dtype=F32) / ROT_DIM)
    return pos.astype(F32)[:, None] * inv[None, :]


def _rope_tables_rowmajor(pos):
    ang = _rope_angles(pos)
    lane = jnp.arange(LANES, dtype=jnp.int32) % QK_HEAD_DIM
    ang = ang[:, lane % ROT_HALF]
    cos, sin = jnp.cos(ang), jnp.sin(ang)
    lo = (lane < ROT_HALF)[None, :]
    hi = ((lane >= ROT_HALF) & (lane < ROT_DIM))[None, :]
    return (jnp.where(lo | hi, cos, 1.0), jnp.where(lo, -sin, 0.0), jnp.where(hi, sin, 0.0))


def _const_spec(shape):
    nd = len(shape)
    return pl.BlockSpec(shape, lambda *_: (0,) * nd, pipeline_mode=pl.Buffered(1))


def kernel(x_prompt, x_sample, c_prompt, c_sample, cache_k, cache_v, state_conv, w_ada, b_ada, g_pre, g_post, w_in,
           w_out, lam_q1, lam_k1, lam_q2, lam_k2, g_subln, w_dw, b_dw, g_ln, b_ln):
    batch, seq, _ = x_prompt.shape
    dec_batch, dec_seq, _ = x_sample.shape
    depth = w_in.shape[0]
    past_len = cache_k.shape[2]
    tile = min(PROMPT_TILE, seq)
    n_tiles = seq // tile
    assert seq % tile == 0 and tile % LANES == 0 and past_len % KEY_TILE == 0 and dec_seq % SUBLANES == 0

    c_all = jnp.concatenate([c_prompt, c_sample], axis=0)
    n_c = c_all.shape[0]
    mod_cols = 512
    mod = pl.pallas_call(
        _mod_kernel,
        grid=(depth, 3 * D_MODEL // mod_cols),
        in_specs=[pl.BlockSpec((n_c, D_MODEL), lambda l, n: (0, 0)),
                  pl.BlockSpec((1, D_MODEL, mod_cols), lambda l, n: (l, 0, n)),
                  pl.BlockSpec((1, 1, mod_cols), lambda l, n: (l, 0, n))],
        out_specs=pl.BlockSpec((1, n_c, mod_cols), lambda l, n: (l, 0, n)),
        out_shape=jax.ShapeDtypeStruct((depth, n_c, 3 * D_MODEL), F32),
        name="adaln_mod",
    )(c_all, w_ada, b_ada.reshape(depth, 1, 3 * D_MODEL))
    mod_p = mod[:, :batch].reshape(depth, batch, 3, D_MODEL)
    mod_s = mod[:, batch:].reshape(depth, dec_batch, 3, D_MODEL)

    w_in_b = w_in.astype(BF16)
    w_t = jnp.swapaxes(jnp.concatenate([w_in_b[:, :, COL_Q:COL_V], w_in_b[:, :, COL_GA:COL_GLU_A]], axis=2), 1, 2)
    w_r = jnp.concatenate([w_in_b[:, :, COL_V:COL_GA], w_in_b[:, :, COL_GLU_A:IN_COLS]], axis=2)
    w_dw_p = jnp.concatenate([w_dw, jnp.zeros((depth, CONV_PAD - CONV_KERNEL, CONV_WIDTH), w_dw.dtype)], axis=1)
    shared = {
        "g_pre": g_pre.reshape(depth, 1, D_MODEL), "g_post": g_post.reshape(depth, 1, D_MODEL),
        "w_t": w_t, "w_r": w_r, "w_out": w_out.astype(BF16),
        "lam": jnp.stack([lam_q1, lam_k1, lam_q2, lam_k2], axis=1),
        "g_subln": g_subln.reshape(depth, 1, V_HEAD_DIM), "g_subln_t": g_subln.reshape(depth, V_HEAD_DIM, 1),
        "w_dw": w_dw_p, "b_dw": b_dw.reshape(depth, 1, CONV_WIDTH),
        "g_ln": g_ln.reshape(depth, 1, CONV_WIDTH), "b_ln": b_ln.reshape(depth, 1, CONV_WIDTH)}

    assert depth == 2, "the prompt call interleaves exactly two layers"
    n_flat = batch * n_tiles
    ang = _rope_angles(jnp.arange(seq, dtype=jnp.int32)).T
    cos_p, sin_p = jnp.cos(ang), jnp.sin(ang)
    params_p = [shared[n] for n in _PROMPT_PRM]

    def tile0(s):
        f = jnp.minimum(s, n_flat - 1)
        return f // n_tiles, f % n_tiles

    def tile1(s):
        f = jnp.maximum(s - 1, 0)
        return f // n_tiles, f % n_tiles

    mod_block = (1, 1, 3, D_MODEL)
    tab_block = (ROT_HALF, tile)
    yp, kp_t, vp, cp = pl.pallas_call(
        functools.partial(_prompt_kernel, n_tiles, n_flat),
        grid=(n_flat + 1,),
        in_specs=[pl.BlockSpec((1, tile, D_MODEL), lambda s: tile0(s) + (0,)),
                  pl.BlockSpec(mod_block, lambda s: (0, tile0(s)[0], 0, 0)),
                  pl.BlockSpec(mod_block, lambda s: (1, tile1(s)[0], 0, 0)),
                  pl.BlockSpec(tab_block, lambda s: (0, tile0(s)[1])),
                  pl.BlockSpec(tab_block, lambda s: (0, tile0(s)[1])),
                  pl.BlockSpec(tab_block, lambda s: (0, tile1(s)[1])),
                  pl.BlockSpec(tab_block, lambda s: (0, tile1(s)[1]))]
                 + [_const_spec(p.shape) for p in params_p],
        out_specs=[pl.BlockSpec((1, tile, D_MODEL), lambda s: tile1(s) + (0,)),
                   pl.BlockSpec((depth, 1, QK_WIDTH, tile), lambda s: (0, tile1(s)[0], 0, tile1(s)[1])),
                   pl.BlockSpec((depth, 1, tile * N_HEADS, V_HEAD_DIM), lambda s: (0,) + tile1(s) + (0,)),
                   pl.BlockSpec((depth, 1, CONV_STATE, CONV_WIDTH), lambda s: (0, tile1(s)[0], 0, 0))],
        out_shape=[jax.ShapeDtypeStruct((batch, seq, D_MODEL), F32),
                   jax.ShapeDtypeStruct((depth, batch, QK_WIDTH, seq), F32),
                   jax.ShapeDtypeStruct((depth, batch, seq * N_HEADS, V_HEAD_DIM), F32),
                   jax.ShapeDtypeStruct((depth, batch, CONV_STATE, CONV_WIDTH), F32)],
        scratch_shapes=[pltpu.VMEM((depth, n_tiles, tile, QK_WIDTH), BF16),
                        pltpu.VMEM((depth, n_tiles, N_HEADS * VT_HEAD_ROWS, tile), BF16),
                        pltpu.VMEM((depth, CONV_PAD + tile, CONV_WIDTH), F32),
                        pltpu.VMEM((2, CONV_WIDTH // LANES, CONV_PAD + tile, LANES), F32),
                        pltpu.VMEM((depth, 2, QK_WIDTH, tile), BF16),
                        pltpu.VMEM((depth, 2 * N_HEADS, VT_HEAD_ROWS, tile), F32),
                        pltpu.VMEM((2 * N_HEADS, 1, tile), F32),
                        pltpu.VMEM((2 * N_HEADS, 1, tile), F32),
                        pltpu.VMEM((2, 2 * N_HEADS, tile, tile), F32),
                        pltpu.VMEM((2 * N_HEADS, tile, tile), BF16),
                        pltpu.VMEM((tile, D_MODEL), F32),
                        pltpu.VMEM((QK_WIDTH, tile), F32),
                        pltpu.VMEM((tile, ATTN_WIDTH), F32),
                        pltpu.VMEM((CONV_STATE, CONV_WIDTH), F32)],
        compiler_params=pltpu.CompilerParams(dimension_semantics=("arbitrary",),
                                             vmem_limit_bytes=VMEM_LIMIT_BYTES),
        name="prompt_layers",
    )(x_prompt, mod_p, mod_p, cos_p, sin_p, cos_p, sin_p, *params_p)
    kp = jnp.transpose(kp_t.reshape(depth, batch, N_HEADS, 2, QK_HEAD_DIM, seq), (0, 1, 5, 2, 3, 4))

    tabs_s = _rope_tables_rowmajor(past_len + jnp.arange(dec_seq, dtype=jnp.int32))
    params_s = [shared[n] for n in _SAMPLE_PRM]

    def layer_spec(shape):
        nd = len(shape)
        return pl.BlockSpec((1,) + tuple(shape[1:]), lambda l, b: (l,) + (0,) * (nd - 1))

    lb_spec = lambda shape: pl.BlockSpec((1, 1) + tuple(shape[2:]), lambda l, b: (l, b) + (0,) * (len(shape) - 2))
    ck_t = jnp.transpose(cache_k, (0, 1, 3, 4, 5, 2)).reshape(depth, dec_batch, QK_WIDTH, past_len)
    cv = cache_v.reshape(depth, dec_batch, past_len * N_HEADS, V_HEAD_DIM)
    n_stage = min(2, past_len // KEY_TILE)
    row_spec = pl.BlockSpec((1, dec_seq, D_MODEL), lambda l, b: (b, 0, 0))
    stab_spec = pl.BlockSpec((dec_seq, LANES), lambda l, b: (0, 0))
    ys, ks, vs, cs = pl.pallas_call(
        functools.partial(_sample_kernel, depth, dec_seq, past_len),
        grid=(depth, dec_batch),
        in_specs=[row_spec, lb_spec(mod_s.shape), stab_spec, stab_spec, stab_spec,
                  lb_spec(ck_t.shape), lb_spec(cv.shape), lb_spec(state_conv.shape)]
                 + [layer_spec(p.shape) for p in params_s],
        out_specs=[pl.BlockSpec((1, dec_seq, D_MODEL), lambda l, b: (jnp.where(l == depth - 1, b, 0), 0, 0)),
                   lb_spec((depth, dec_batch, dec_seq, QK_WIDTH)),
                   lb_spec((depth, dec_batch, dec_seq, ATTN_WIDTH)), lb_spec(state_conv.shape)],
        out_shape=[jax.ShapeDtypeStruct((dec_batch, dec_seq, D_MODEL), F32),
                   jax.ShapeDtypeStruct((depth, dec_batch, dec_seq, QK_WIDTH), F32),
                   jax.ShapeDtypeStruct((depth, dec_batch, dec_seq, ATTN_WIDTH), F32),
                   jax.ShapeDtypeStruct(state_conv.shape, F32)],
        scratch_shapes=[pltpu.VMEM((past_len // KEY_TILE, QK_WIDTH, KEY_TILE), BF16),
                        pltpu.VMEM((past_len, ATTN_WIDTH), BF16),
                        pltpu.VMEM((1, CONV_PAD + dec_seq, CONV_WIDTH), F32),
                        pltpu.VMEM((2, CONV_WIDTH // LANES, CONV_PAD + dec_seq, LANES), F32),
                        pltpu.VMEM((2, dec_seq, QK_WIDTH), BF16),
                        pltpu.VMEM((2 * N_HEADS, dec_seq, V_HEAD_DIM), F32),
                        pltpu.VMEM((2 * N_HEADS, dec_seq, 1), F32), pltpu.VMEM((2 * N_HEADS, dec_seq, 1), F32),
                        pltpu.VMEM((n_stage, 2 * N_HEADS, dec_seq, 1), F32),
                        pltpu.VMEM((n_stage, 2 * N_HEADS, dec_seq, KEY_TILE), F32),
                        pltpu.VMEM((n_stage, 2 * N_HEADS, dec_seq, KEY_TILE), BF16),
                        pltpu.VMEM((dec_seq, D_MODEL), F32), pltpu.VMEM((dec_batch, dec_seq, D_MODEL), F32)],
        compiler_params=pltpu.CompilerParams(dimension_semantics=("arbitrary", "arbitrary"),
                                             vmem_limit_bytes=VMEM_LIMIT_BYTES),
        name="sample_layers",
    )(x_sample, mod_s, *tabs_s, ck_t, cv, state_conv, *params_s)

    return (yp, ys, kp, vp.reshape(depth, batch, seq, N_HEADS, V_HEAD_DIM), cp,
            ks.reshape(depth, dec_batch, dec_seq, N_HEADS, 2, QK_HEAD_DIM),
            vs.reshape(depth, dec_batch, dec_seq, N_HEADS, V_HEAD_DIM), cs)
```
